```python
import math
import jax, jax.numpy as jnp
from jax import lax
import numpy as np

D_MODEL = 1024
BATCH = 2
SEQ = 8192
DEPTH = 2

CTX_LEN = 256
GRID_W = 64
N_HEADS = 8
HEAD_DIM = 64
V_DIM = 2 * HEAD_DIM
QK_W = N_HEADS * 2 * HEAD_DIM
ATTN_WIDTH = N_HEADS * V_DIM
N_FGROUPS = 4
FGROUP_DIM = 128
FOURIER_WIDTH = N_FGROUPS * FGROUP_DIM
F_OFF = 2 * QK_W + ATTN_WIDTH
G_OFF = F_OFF + FOURIER_WIDTH
IN_WIDTH = G_OFF + 2 * D_MODEL
D_FF = 2816
AXIS_DIM = HEAD_DIM // 2
ROPE_BASE = 10000.0
Q_BLOCK = 128
N_MOD = 9
EPS = 1e-6

kernel_name = "hybrid_diffattn_fnet_macaron_dit"


def rms_norm(x, g):
    xf = x.astype(jnp.float32)
    y = xf * lax.rsqrt(jnp.mean(xf * xf, axis=-1, keepdims=True) + EPS)
    return (y * g.astype(jnp.float32)).astype(x.dtype)


def adaln_mod(cond, w_mod, b_mod):
    m = jax.nn.silu(cond) @ w_mod + b_mod
    return m.reshape(cond.shape[:-1] + (N_MOD, D_MODEL))


def modulate(x, m, i):
    shift = m[:, 3 * i][:, None]
    scale = m[:, 3 * i + 1][:, None]
    return x * (1 + scale) + shift


def swiglu(u, w_in, w_out):
    a, b = jnp.split(u @ w_in, 2, axis=-1)
    return (jax.nn.silu(a) * b) @ w_out


def ffn_sublayer(x, m, i, g, w_in, w_out):
    u = modulate(rms_norm(x, g), m, i)
    return x + 0.5 * m[:, 3 * i + 2][:, None] * swiglu(u, w_in, w_out)


def split_qkv(p):
    B, S = p.shape[:2]
    q = p[..., :QK_W].reshape(B, S, N_HEADS, 2, HEAD_DIM)
    k = p[..., QK_W:2 * QK_W].reshape(B, S, N_HEADS, 2, HEAD_DIM)
    v = p[..., 2 * QK_W:F_OFF].reshape(B, S, N_HEADS, V_DIM)
    return q, k, v


def rope_tables(S):
    ROWS = S // GRID_W
    row = jnp.broadcast_to(jnp.arange(ROWS, dtype=jnp.int32)[:, None], (ROWS, GRID_W)).reshape(S)
    col = jnp.broadcast_to(jnp.arange(GRID_W, dtype=jnp.int32)[None, :], (ROWS, GRID_W)).reshape(S)
    inv = 1.0 / (ROPE_BASE ** (jnp.arange(0, AXIS_DIM, 2, dtype=jnp.float32) / AXIS_DIM))

    def tab(pos):
        ang = pos.astype(jnp.float32)[:, None] * inv[None, :]
        ang = jnp.concatenate([ang, ang], axis=-1)
        return jnp.cos(ang), jnp.sin(ang)

    cr, sr = tab(row)
    cc, sc = tab(col)
    return cr, sr, cc, sc


def rotate(xh, cos, sin):
    cos = cos[None, :, None, None, :].astype(xh.dtype)
    sin = sin[None, :, None, None, :].astype(xh.dtype)
    x1, x2 = jnp.split(xh, 2, axis=-1)
    return xh * cos + jnp.concatenate([-x2, x1], axis=-1) * sin


def apply_rope2d(x, tabs):
    cr, sr, cc, sc = tabs
    return jnp.concatenate([rotate(x[..., :AXIS_DIM], cr, sr),
                            rotate(x[..., AXIS_DIM:], cc, sc)], axis=-1)


def diff_attend(q, k, v, lam):
    s = jnp.einsum("bqhcd,bkhcd->bhcqk", q, k).astype(jnp.float32) * (HEAD_DIM ** -0.5)
    p = jax.nn.softmax(s, axis=-1)
    a = p[:, :, 0] - lam * p[:, :, 1]
    return jnp.einsum("bhqk,bkhe->bqhe", a.astype(v.dtype), v)


def latent_diff_attention(q, k_all, v_all, lam):
    B, S = q.shape[:2]
    nb = S // Q_BLOCK
    qb = jnp.moveaxis(q.reshape(B, nb, Q_BLOCK, N_HEADS, 2, HEAD_DIM), 1, 0)
    out = lax.map(lambda blk: diff_attend(blk, k_all, v_all, lam), qb)
    return jnp.moveaxis(out, 0, 1).reshape(B, S, N_HEADS, V_DIM)


def fourier_mix(f):
    B, S = f.shape[:2]
    fg = f.reshape(B, S, N_FGROUPS, FGROUP_DIM).astype(jnp.float32)
    y = jnp.fft.fft2(fg, axes=(1, 3), norm="ortho").real
    return y.reshape(B, S, FOURIER_WIDTH).astype(f.dtype)


def mixer_merge(attn_heads, f, g_logits, lam_init, subln_g, w_attn_out, w_four_out, b_gate, w_o):
    B, S = attn_heads.shape[:2]
    a = (rms_norm(attn_heads, subln_g) * (1 - lam_init)).reshape(B, S, ATTN_WIDTH) @ w_attn_out
    fr = fourier_mix(f) @ w_four_out
    ga, gf = jnp.split(g_logits, 2, axis=-1)
    merged = jax.nn.sigmoid(ga + b_gate[0]) * a + jax.nn.sigmoid(gf + b_gate[1]) * fr
    return merged @ w_o


def setup_inputs(seed: int = 0) -> dict:
    key = jax.random.key(seed)
    ks = jax.random.split(key, 20)
    f32 = jnp.float32
    D = D_MODEL

    def nrm(k, shape, scale):
        return jax.random.normal(k, shape, f32) * scale

    return {
        "x": nrm(ks[0], (BATCH, SEQ, D), 1.0),
        "c": nrm(ks[1], (BATCH, D), 1.0),
        "ctx": nrm(ks[2], (BATCH, CTX_LEN, D), 1.0),
        "c_ctx": nrm(ks[3], (D,), 1.0),
        "w_mod": nrm(ks[4], (DEPTH, D, N_MOD * D), 0.5 * D ** -0.5),
        "b_mod": nrm(ks[5], (DEPTH, N_MOD * D), 0.01),
        "norm_g": 1.0 + nrm(ks[6], (DEPTH, 3, D), 0.1),
        "ffn_w_in": nrm(ks[7], (DEPTH, 2, D, 2 * D_FF), D ** -0.5),
        "ffn_w_out": nrm(ks[8], (DEPTH, 2, D_FF, D), D_FF ** -0.5),
        "w_in": nrm(ks[9], (DEPTH, D, IN_WIDTH), D ** -0.5),
        "b_gate": nrm(ks[10], (DEPTH, 2, D), 0.01),
        "lam_params": nrm(ks[11], (DEPTH, 4, HEAD_DIM), 0.1),
        "subln_g": 1.0 + nrm(ks[12], (DEPTH, V_DIM), 0.1),
        "w_attn_out": nrm(ks[13], (DEPTH, ATTN_WIDTH, D), ATTN_WIDTH ** -0.5),
        "w_four_out": nrm(ks[14], (DEPTH, FOURIER_WIDTH, D), FOURIER_WIDTH ** -0.5),
        "w_o": nrm(ks[15], (DEPTH, D, D), D ** -0.5),
        "final_g": 1.0 + nrm(ks[16], (D,), 0.1),
    }


def reference(x, c, ctx, c_ctx, w_mod, b_mod, norm_g, ffn_w_in, ffn_w_out, w_in, b_gate,
              lam_params, subln_g, w_attn_out, w_four_out, w_o, final_g):
    S = x.shape[1]
    tabs = rope_tables(S)
    x_lat = x
    x_ctx = ctx
    for l in range(DEPTH):
        last = l == DEPTH - 1
        lam_init = 0.8 - 0.6 * math.exp(-0.3 * l)
        lp = lam_params[l].astype(jnp.float32)
        lam = jnp.exp(jnp.sum(lp[0] * lp[1])) - jnp.exp(jnp.sum(lp[2] * lp[3])) + lam_init

        m_lat = adaln_mod(c, w_mod[l], b_mod[l])
        m_ctx = adaln_mod(c_ctx, w_mod[l], b_mod[l])[None]

        x_lat = ffn_sublayer(x_lat, m_lat, 0, norm_g[l, 0], ffn_w_in[l, 0], ffn_w_out[l, 0])
        x_ctx = ffn_sublayer(x_ctx, m_ctx, 0, norm_g[l, 0], ffn_w_in[l, 0], ffn_w_out[l, 0])

        u_lat = modulate(rms_norm(x_lat, norm_g[l, 1]), m_lat, 1)
        u_ctx = modulate(rms_norm(x_ctx, norm_g[l, 1]), m_ctx, 1)
        p_lat = u_lat @ w_in[l]
        q_l, k_l, v_l = split_qkv(p_lat)
        q_l = apply_rope2d(q_l, tabs)
        k_l = apply_rope2d(k_l, tabs)
        if last:
            kv_c = u_ctx @ w_in[l][:, QK_W:F_OFF]
            B = kv_c.shape[0]
            k_c = kv_c[..., :QK_W].reshape(B, CTX_LEN, N_HEADS, 2, HEAD_DIM)
            v_c = kv_c[..., QK_W:].reshape(B, CTX_LEN, N_HEADS, V_DIM)
        else:
            p_ctx = u_ctx @ w_in[l]
            q_c, k_c, v_c = split_qkv(p_ctx)

        k_all = jnp.concatenate([k_c, k_l], axis=1)
        v_all = jnp.concatenate([v_c, v_l], axis=1)
        attn_l = latent_diff_attention(q_l, k_all, v_all, lam)
        mix_l = mixer_merge(attn_l, p_lat[..., F_OFF:G_OFF], p_lat[..., G_OFF:], lam_init,
                            subln_g[l], w_attn_out[l], w_four_out[l], b_gate[l], w_o[l])
        x_lat = x_lat + m_lat[:, 5][:, None] * mix_l

        if not last:
            attn_c = diff_attend(q_c, k_c, v_c, lam)
            mix_c = mixer_merge(attn_c, p_ctx[..., F_OFF:G_OFF], p_ctx[..., G_OFF:], lam_init,
                                subln_g[l], w_attn_out[l], w_four_out[l], b_gate[l], w_o[l])
            x_ctx = x_ctx + m_ctx[:, 5][:, None] * mix_c
            x_ctx = ffn_sublayer(x_ctx, m_ctx, 2, norm_g[l, 2], ffn_w_in[l, 1], ffn_w_out[l, 1])

        x_lat = ffn_sublayer(x_lat, m_lat, 2, norm_g[l, 2], ffn_w_in[l, 1], ffn_w_out[l, 1])

    return rms_norm(x_lat, final_g)
```

```python
import functools
import math

import jax
import jax.numpy as jnp
from jax import lax
from jax.experimental import pallas as pl
from jax.experimental.pallas import tpu as pltpu

D_MODEL = 1024
BATCH = 2
SEQ = 8192
DEPTH = 2
CTX_LEN = 256
GRID_W = 64
N_HEADS = 8
HEAD_DIM = 64
V_DIM = 2 * HEAD_DIM
QK_W = N_HEADS * 2 * HEAD_DIM
ATTN_WIDTH = N_HEADS * V_DIM
N_FGROUPS = 4
FGROUP_DIM = 128
FOURIER_WIDTH = N_FGROUPS * FGROUP_DIM
F_OFF = 2 * QK_W + ATTN_WIDTH
G_OFF = F_OFF + FOURIER_WIDTH
IN_WIDTH = G_OFF + 2 * D_MODEL
D_FF = 2816
AXIS_DIM = HEAD_DIM // 2
ROPE_BASE = 10000.0
N_MOD = 9
EPS = 1e-6

R_LAT = BATCH * SEQ
R_ALL = R_LAT + BATCH * CTX_LEN
TM = 512
N_LAT_TILES = R_LAT // TM
N_ALL_TILES = R_ALL // TM
FF_CHUNK = 256
N_FF_CHUNKS = D_FF // FF_CHUNK
TQ = 256
TK = 512
N_LAT_QT = SEQ // TQ
DFT_A = 64
DFT_B = 128
NS2 = 8
NK1 = 8
VMEM_LIMIT = 56 * 1024 * 1024

F32 = jnp.float32
BF16 = jnp.bfloat16


def _dot(a, b):
    return jnp.dot(a, b, preferred_element_type=F32)


def _sigmoid(x):
    return 1.0 / (1.0 + jnp.exp(-x))


def _mod_row(i):
    return jnp.minimum(i // (SEQ // TM), 2)


def _norm_mod(x, g, shift, scale):
    var = jnp.mean(x * x, axis=-1, keepdims=True)
    y = x * lax.rsqrt(var + EPS) * g
    return y * (1.0 + scale) + shift


def _mod_kernel(c_ref, w_ref, b_ref, o_ref):
    c = c_ref[...]
    s = c * _sigmoid(c)
    o_ref[0] = jnp.dot(s, w_ref[0], preferred_element_type=F32,
                       precision=lax.Precision.HIGHEST) + b_ref[0]


def _modulation(conds, w_mod, b_mod):
    tn = D_MODEL
    out = pl.pallas_call(
        _mod_kernel,
        grid=(DEPTH, N_MOD),
        in_specs=[
            pl.BlockSpec((8, D_MODEL), lambda l, j: (0, 0)),
            pl.BlockSpec((1, D_MODEL, tn), lambda l, j: (l, 0, j)),
            pl.BlockSpec((1, 1, tn), lambda l, j: (l, 0, j)),
        ],
        out_specs=pl.BlockSpec((1, 8, tn), lambda l, j: (l, 0, j)),
        out_shape=jax.ShapeDtypeStruct((DEPTH, 8, N_MOD * D_MODEL), F32),
        compiler_params=pltpu.CompilerParams(
            dimension_semantics=("arbitrary", "arbitrary"), vmem_limit_bytes=VMEM_LIMIT),
        name="modulation",
    )(conds, w_mod, b_mod.reshape(DEPTH, 1, N_MOD * D_MODEL))
    return out.reshape(DEPTH, 8, N_MOD, D_MODEL)


def _ffn_kernel(*refs, sub, final):
    if final:
        x_ref, m_ref, g_ref, win_ref, wout_ref, fg_ref, o_ref, u_ref, acc_ref = refs
    else:
        x_ref, m_ref, g_ref, win_ref, wout_ref, o_ref, u_ref, acc_ref = refs
    x = x_ref[...]
    m = m_ref[0, 0]
    shift = m[3 * sub:3 * sub + 1]
    scale = m[3 * sub + 1:3 * sub + 2]
    gate = m[3 * sub + 2:3 * sub + 3]
    u_ref[...] = _norm_mod(x, g_ref[...], shift, scale).astype(BF16)
    acc_ref[...] = jnp.zeros_like(acc_ref)

    def body(c, carry):
        u = u_ref[...]
        a = _dot(u, win_ref[c])
        b = _dot(u, win_ref[c + N_FF_CHUNKS])
        h = (a * _sigmoid(a) * b).astype(BF16)
        acc_ref[...] += _dot(h, wout_ref[c])
        return carry

    lax.fori_loop(0, N_FF_CHUNKS, body, 0)
    y = x + 0.5 * gate * acc_ref[...]
    if final:
        var = jnp.mean(y * y, axis=-1, keepdims=True)
        y = y * lax.rsqrt(var + EPS) * fg_ref[...]
    o_ref[...] = y


def _ffn(x, m_l, g, w_in_c, w_out_c, *, sub, n_tiles, final_g=None):
    final = final_g is not None
    const = dict(pipeline_mode=pl.Buffered(1))
    in_specs = [
        pl.BlockSpec((TM, D_MODEL), lambda i: (i, 0)),
        pl.BlockSpec((1, 1, N_MOD, D_MODEL), lambda i: (0, _mod_row(i), 0, 0)),
        pl.BlockSpec((1, D_MODEL), lambda i: (0, 0)),
        pl.BlockSpec((2 * N_FF_CHUNKS, D_MODEL, FF_CHUNK), lambda i: (0, 0, 0), **const),
        pl.BlockSpec((N_FF_CHUNKS, FF_CHUNK, D_MODEL), lambda i: (0, 0, 0), **const),
    ]
    args = [x, m_l, g.reshape(1, D_MODEL), w_in_c, w_out_c]
    if final:
        in_specs.append(pl.BlockSpec((1, D_MODEL), lambda i: (0, 0)))
        args.append(final_g.reshape(1, D_MODEL))
    return pl.pallas_call(
        functools.partial(_ffn_kernel, sub=sub, final=final),
        grid=(n_tiles,),
        in_specs=in_specs,
        out_specs=pl.BlockSpec((TM, D_MODEL), lambda i: (i, 0)),
        out_shape=jax.ShapeDtypeStruct((n_tiles * TM, D_MODEL), F32),
        scratch_shapes=[pltpu.VMEM((TM, D_MODEL), BF16), pltpu.VMEM((TM, D_MODEL), F32)],
        compiler_params=pltpu.CompilerParams(
            dimension_semantics=("arbitrary",), vmem_limit_bytes=VMEM_LIMIT),
        name="ffn",
    )(*args)


def _proj_kernel(x_ref, m_ref, g_ref, w_ref, cos_ref, sa_ref, sb_ref, wc_ref, bg_ref,
                 q_ref, k_ref, v_ref, f_ref, ga_ref, gf_ref, u_ref):
    x = x_ref[...]
    m = m_ref[0, 0]
    u_ref[...] = _norm_mod(x, g_ref[...], m[3:4], m[4:5]).astype(BF16)
    cw = 256
    cos = jnp.concatenate([cos_ref[...]] * (cw // 128), axis=1)
    sa = jnp.concatenate([sa_ref[...]] * (cw // 128), axis=1)
    sb = jnp.concatenate([sb_ref[...]] * (cw // 128), axis=1)

    def col(off):
        return _dot(u_ref[...], w_ref[:, off:off + cw])

    for out_ref, base, qscale in ((q_ref, 0, HEAD_DIM ** -0.5), (k_ref, QK_W, 1.0)):
        for j in range(QK_W // cw):
            p = col(base + j * cw)
            r = p * cos + pltpu.roll(p, cw - 16, 1) * sa + pltpu.roll(p, 16, 1) * sb
            if qscale != 1.0:
                r = r * qscale
            out_ref[:, j * cw:(j + 1) * cw] = r.astype(BF16)
    for j in range(ATTN_WIDTH // cw):
        v_ref[:, j * cw:(j + 1) * cw] = col(2 * QK_W + j * cw).astype(BF16)
    for j in range(FOURIER_WIDTH // cw):
        p = col(F_OFF + j * cw).astype(BF16)
        for t in range(cw // FGROUP_DIM):
            grp = j * (cw // FGROUP_DIM) + t
            r = _dot(p[:, t * FGROUP_DIM:(t + 1) * FGROUP_DIM], wc_ref[...])
            f_ref[:, grp * FGROUP_DIM:(grp + 1) * FGROUP_DIM] = r[:, :FGROUP_DIM].astype(BF16)
            f_ref[:, FOURIER_WIDTH + grp * FGROUP_DIM:FOURIER_WIDTH + (grp + 1) * FGROUP_DIM] = (
                r[:, FGROUP_DIM:].astype(BF16))
    for out_ref, base, brow in ((ga_ref, G_OFF, 0), (gf_ref, G_OFF + D_MODEL, 1)):
        for j in range(D_MODEL // cw):
            p = col(base + j * cw) + bg_ref[brow:brow + 1, j * cw:(j + 1) * cw]
            out_ref[:, j * cw:(j + 1) * cw] = _sigmoid(p).astype(BF16)


def _proj(x, m_l, g, w_in_b, tabs, wc, b_gate):
    cos, sa, sb = tabs
    const = dict(pipeline_mode=pl.Buffered(1))
    row = lambda i: (i, 0)
    fix = lambda i: (0, 0)
    wide = jax.ShapeDtypeStruct((R_ALL, D_MODEL), BF16)
    return pl.pallas_call(
        _proj_kernel,
        grid=(N_ALL_TILES,),
        in_specs=[
            pl.BlockSpec((TM, D_MODEL), row),
            pl.BlockSpec((1, 1, N_MOD, D_MODEL), lambda i: (0, _mod_row(i), 0, 0)),
            pl.BlockSpec((1, D_MODEL), fix),
            pl.BlockSpec((D_MODEL, IN_WIDTH), fix, **const),
            pl.BlockSpec((TM, 128), row),
            pl.BlockSpec((TM, 128), row),
            pl.BlockSpec((TM, 128), row),
            pl.BlockSpec((FGROUP_DIM, 2 * FGROUP_DIM), fix),
            pl.BlockSpec((2, D_MODEL), fix),
        ],
        out_specs=[pl.BlockSpec((TM, D_MODEL), row)] * 6,
        out_shape=[wide] * 6,
        scratch_shapes=[pltpu.VMEM((TM, D_MODEL), BF16)],
        compiler_params=pltpu.CompilerParams(
            dimension_semantics=("arbitrary",), vmem_limit_bytes=VMEM_LIMIT),
        name="proj",
    )(x, m_l, g.reshape(1, D_MODEL), w_in_b, cos, sa, sb, wc, b_gate)


def _attn_kernel(lp_ref, sg_ref, q_ref, kc_ref, vc_ref, kl_ref, vl_ref, o_ref, acc_ref, *, lam_init):
    qt = pl.program_id(2)
    q = q_ref[...]
    lane = lax.broadcasted_iota(jnp.int32, q.shape, 1)
    zero = jnp.zeros_like(q)
    qs = jnp.concatenate([jnp.where(lane < HEAD_DIM, q, zero), jnp.where(lane >= HEAD_DIM, q, zero)], axis=0)

    def scores(kb):
        return lax.dot_general(qs, kb, (((1,), (1,)), ((), ())), preferred_element_type=F32)

    s = scores(kc_ref[...])
    m0 = jnp.max(s, axis=-1, keepdims=True)
    p = jnp.exp(s - m0)
    l0 = jnp.sum(p, axis=-1, keepdims=True)
    acc_ref[...] = _dot(p.astype(BF16), vc_ref[...])

    def body(i, carry):
        m, l = carry
        off = pl.multiple_of(i * TK, TK)
        s = scores(kl_ref[pl.ds(off, TK), :])
        m_new = jnp.maximum(m, jnp.max(s, axis=-1, keepdims=True))
        alpha = jnp.exp(m - m_new)
        p = jnp.exp(s - m_new)
        l_new = alpha * l + jnp.sum(p, axis=-1, keepdims=True)
        acc_ref[...] = alpha * acc_ref[...] + _dot(p.astype(BF16), vl_ref[pl.ds(off, TK), :])
        return m_new, l_new

    n_blocks = jnp.where(qt < N_LAT_QT, SEQ // TK, 0)
    _, l = lax.fori_loop(0, n_blocks, body, (m0, l0))

    lp = lp_ref[...]
    lam = (jnp.exp(jnp.sum(lp[0:1] * lp[1:2], axis=-1, keepdims=True))
           - jnp.exp(jnp.sum(lp[2:3] * lp[3:4], axis=-1, keepdims=True)) + lam_init)
    o = acc_ref[...] / l
    d = o[:TQ] - lam * o[TQ:]
    var = jnp.mean(d * d, axis=-1, keepdims=True)
    y = d * lax.rsqrt(var + EPS) * sg_ref[...] * (1.0 - lam_init)
    o_ref[...] = y.astype(BF16)


def _attention(q, k, v, lam_params_l, subln_g_l, *, lam_init, with_ctx_queries):
    n_qt = N_LAT_QT + (1 if with_ctx_queries else 0)
    ctx_blk0 = R_LAT // CTX_LEN

    def q_map(b, h, t):
        return (jnp.where(t < N_LAT_QT, b * N_LAT_QT + t, ctx_blk0 + b), h)

    rows = R_ALL if with_ctx_queries else R_LAT
    return pl.pallas_call(
        functools.partial(_attn_kernel, lam_init=lam_init),
        grid=(BATCH, N_HEADS, n_qt),
        in_specs=[
            pl.BlockSpec((4, HEAD_DIM), lambda b, h, t: (0, 0)),
            pl.BlockSpec((1, V_DIM), lambda b, h, t: (0, 0)),
            pl.BlockSpec((TQ, V_DIM), q_map),
            pl.BlockSpec((CTX_LEN, V_DIM), lambda b, h, t: (ctx_blk0 + b, h)),
            pl.BlockSpec((CTX_LEN, V_DIM), lambda b, h, t: (ctx_blk0 + b, h)),
            pl.BlockSpec((SEQ, V_DIM), lambda b, h, t: (b, h)),
            pl.BlockSpec((SEQ, V_DIM), lambda b, h, t: (b, h)),
        ],
        out_specs=pl.BlockSpec((TQ, V_DIM), q_map),
        out_shape=jax.ShapeDtypeStruct((rows, ATTN_WIDTH), BF16),
        scratch_shapes=[pltpu.VMEM((2 * TQ, V_DIM), F32)],
        compiler_params=pltpu.CompilerParams(
            dimension_semantics=("arbitrary", "arbitrary", "arbitrary"), vmem_limit_bytes=VMEM_LIMIT),
        name="diff_attn",
    )(lam_params_l, subln_g_l.reshape(1, V_DIM), q, k, v, k, v)


def _dft1_kernel(x_ref, t_ref, o_ref):
    w = 2 * FOURIER_WIDTH
    for j in range(NS2):
        xs = jnp.concatenate([x_ref[:, j * w:j * w + FOURIER_WIDTH],
                              x_ref[:, j * w + FOURIER_WIDTH:(j + 1) * w]], axis=0)
        r = _dot(t_ref[j], xs)
        o_ref[:, j * w:j * w + FOURIER_WIDTH] = r[:DFT_A].astype(BF16)
        o_ref[:, j * w + FOURIER_WIDTH:(j + 1) * w] = r[DFT_A:].astype(BF16)


def _dft2_kernel(x_ref, w_ref, o_ref, *, n, nk, scale):
    for j in range(nk):
        xs = jnp.concatenate([x_ref[j * n:(j + 1) * n, :FOURIER_WIDTH],
                              x_ref[j * n:(j + 1) * n, FOURIER_WIDTH:]], axis=0)
        r = _dot(w_ref[...], xs) * scale
        o_ref[:, j * FOURIER_WIDTH:(j + 1) * FOURIER_WIDTH] = r.astype(BF16)


def _fourier_latent(f_all, t1, w2):
    width = DFT_B * 2 * FOURIER_WIDTH
    x2 = pl.pallas_call(
        _dft1_kernel,
        grid=(BATCH, DFT_B // NS2),
        in_specs=[
            pl.BlockSpec((DFT_A, NS2 * 2 * FOURIER_WIDTH), lambda b, j: (b, j)),
            pl.BlockSpec((NS2, 2 * DFT_A, 2 * DFT_A), lambda b, j: (j, 0, 0)),
        ],
        out_specs=pl.BlockSpec((DFT_A, NS2 * 2 * FOURIER_WIDTH), lambda b, j: (b, j)),
        out_shape=jax.ShapeDtypeStruct((BATCH * DFT_A, width), BF16),
        compiler_params=pltpu.CompilerParams(
            dimension_semantics=("arbitrary", "arbitrary"), vmem_limit_bytes=VMEM_LIMIT),
        name="dft_stage1",
    )(f_all.reshape(R_ALL // DFT_B, width), t1)
    y = pl.pallas_call(
        functools.partial(_dft2_kernel, n=DFT_B, nk=NK1, scale=(SEQ * FGROUP_DIM) ** -0.5),
        grid=(BATCH, DFT_A // NK1),
        in_specs=[
            pl.BlockSpec((NK1 * DFT_B, 2 * FOURIER_WIDTH), lambda b, j: (b * (DFT_A // NK1) + j, 0)),
            pl.BlockSpec((DFT_B, 2 * DFT_B), lambda b, j: (0, 0)),
        ],
        out_specs=pl.BlockSpec((DFT_B, NK1 * FOURIER_WIDTH), lambda b, j: (b, j)),
        out_shape=jax.ShapeDtypeStruct((BATCH * DFT_B, DFT_A * FOURIER_WIDTH), BF16),
        compiler_params=pltpu.CompilerParams(
            dimension_semantics=("arbitrary", "arbitrary"), vmem_limit_bytes=VMEM_LIMIT),
        name="dft_stage2",
    )(x2.reshape(R_LAT, 2 * FOURIER_WIDTH), w2)
    return y.reshape(R_LAT, FOURIER_WIDTH)


def _fourier_ctx(f_all, wctx):
    ctx_blk0 = R_LAT // CTX_LEN
    return pl.pallas_call(
        functools.partial(_dft2_kernel, n=CTX_LEN, nk=1, scale=(CTX_LEN * FGROUP_DIM) ** -0.5),
        grid=(BATCH,),
        in_specs=[
            pl.BlockSpec((CTX_LEN, 2 * FOURIER_WIDTH), lambda b: (ctx_blk0 + b, 0)),
            pl.BlockSpec((CTX_LEN, 2 * CTX_LEN), lambda b: (0, 0)),
        ],
        out_specs=pl.BlockSpec((CTX_LEN, FOURIER_WIDTH), lambda b: (b, 0)),
        out_shape=jax.ShapeDtypeStruct((BATCH * CTX_LEN, FOURIER_WIDTH), BF16),
        compiler_params=pltpu.CompilerParams(
            dimension_semantics=("arbitrary",), vmem_limit_bytes=VMEM_LIMIT),
        name="dft_ctx",
    )(f_all, wctx)


def _merge_kernel(x_ref, m_ref, a_ref, f_ref, ga_ref, gf_ref, wa_ref, wf_ref, wo_ref, o_ref):
    a = _dot(a_ref[...], wa_ref[...])
    fr = _dot(f_ref[...], wf_ref[...])
    merged = ga_ref[...].astype(F32) * a + gf_ref[...].astype(F32) * fr
    mix = _dot(merged.astype(BF16), wo_ref[...])
    o_ref[...] = x_ref[...] + m_ref[0, 0][5:6] * mix


def _merge(x, m_l, attn, four, ga, gf, wa, wf, wo, *, n_tiles):
    const = dict(pipeline_mode=pl.Buffered(1))
    row = lambda i: (i, 0)
    fix = lambda i: (0, 0)
    return pl.pallas_call(
        _merge_kernel,
        grid=(n_tiles,),
        in_specs=[
            pl.BlockSpec((TM, D_MODEL), row),
            pl.BlockSpec((1, 1, N_MOD, D_MODEL), lambda i: (0, _mod_row(i), 0, 0)),
            pl.BlockSpec((TM, ATTN_WIDTH), row),
            pl.BlockSpec((TM, FOURIER_WIDTH), row),
            pl.BlockSpec((TM, D_MODEL), row),
            pl.BlockSpec((TM, D_MODEL), row),
            pl.BlockSpec((ATTN_WIDTH, D_MODEL), fix, **const),
            pl.BlockSpec((FOURIER_WIDTH, D_MODEL), fix, **const),
            pl.BlockSpec((D_MODEL, D_MODEL), fix, **const),
        ],
        out_specs=pl.BlockSpec((TM, D_MODEL), row),
        out_shape=jax.ShapeDtypeStruct((n_tiles * TM, D_MODEL), F32),
        compiler_params=pltpu.CompilerParams(
            dimension_semantics=("arbitrary",), vmem_limit_bytes=VMEM_LIMIT),
        name="merge",
    )(x, m_l, attn, four, ga, gf, wa, wf, wo)


def _rope_tables():
    s = jnp.arange(SEQ, dtype=jnp.int32)
    lane = jnp.arange(128, dtype=jnp.int32) % HEAD_DIM
    inv = 1.0 / (ROPE_BASE ** ((2 * (lane % (AXIS_DIM // 2))).astype(F32) / AXIS_DIM))
    pos = jnp.where(lane[None, :] < AXIS_DIM, (s // GRID_W)[:, None], (s % GRID_W)[:, None]).astype(F32)
    ang = pos * inv[None, :]
    cos, sin = jnp.cos(ang), jnp.sin(ang)
    first = (lane % AXIS_DIM) < (AXIS_DIM // 2)
    sa = jnp.where(first[None, :], -sin, 0.0)
    sb = jnp.where(first[None, :], 0.0, sin)
    n_ctx = BATCH * CTX_LEN

    def full(t, fill):
        return jnp.concatenate([t] * BATCH + [jnp.full((n_ctx, 128), fill, F32)], axis=0)

    return full(cos, 1.0), full(sa, 0.0), full(sb, 0.0)


def _angles(num, den):
    return (2.0 * math.pi / den) * (num % den).astype(F32)


def _dft_tables():
    c = jnp.arange(FGROUP_DIM, dtype=jnp.int32)
    ang = _angles(c[:, None] * c[None, :], FGROUP_DIM)
    wc = jnp.concatenate([jnp.cos(ang), -jnp.sin(ang)], axis=1).astype(BF16)

    s2 = jnp.arange(DFT_B, dtype=jnp.int32)[:, None, None]
    k1 = jnp.arange(DFT_A, dtype=jnp.int32)[None, :, None]
    s1 = jnp.arange(DFT_A, dtype=jnp.int32)[None, None, :]
    ang = _angles(k1 * (DFT_B * s1 + s2), SEQ)
    cs, sn = jnp.cos(ang), jnp.sin(ang)
    t1 = jnp.concatenate([jnp.concatenate([cs, sn], axis=2),
                          jnp.concatenate([-sn, cs], axis=2)], axis=1).astype(BF16)

    b = jnp.arange(DFT_B, dtype=jnp.int32)
    ang = _angles(b[:, None] * b[None, :], DFT_B)
    w2 = jnp.concatenate([jnp.cos(ang), jnp.sin(ang)], axis=1).astype(BF16)

    n = jnp.arange(CTX_LEN, dtype=jnp.int32)
    ang = _angles(n[:, None] * n[None, :], CTX_LEN)
    wctx = jnp.concatenate([jnp.cos(ang), jnp.sin(ang)], axis=1).astype(BF16)
    return wc, t1, w2, wctx


def kernel(x, c, ctx, c_ctx, w_mod, b_mod, norm_g, ffn_w_in, ffn_w_out, w_in, b_gate,
           lam_params, subln_g, w_attn_out, w_four_out, w_o, final_g):
    xs = jnp.concatenate([x.reshape(R_LAT, D_MODEL), ctx.reshape(BATCH * CTX_LEN, D_MODEL)], axis=0)
    conds = jnp.concatenate([c, c_ctx[None, :], jnp.zeros((8 - BATCH - 1, D_MODEL), F32)], axis=0)
    m_all = _modulation(conds, w_mod, b_mod)
    tabs = _rope_tables()
    wc, t1, w2, wctx = _dft_tables()

    for l in range(DEPTH):
        last = l == DEPTH - 1
        lam_init = 0.8 - 0.6 * math.exp(-0.3 * l)
        m_l = m_all[l:l + 1]
        w_ffn_in = [ffn_w_in[l, i].astype(BF16).reshape(D_MODEL, 2 * N_FF_CHUNKS, FF_CHUNK).transpose(1, 0, 2)
                    for i in range(2)]
        w_ffn_out = [ffn_w_out[l, i].astype(BF16).reshape(N_FF_CHUNKS, FF_CHUNK, D_MODEL) for i in range(2)]

        xs = _ffn(xs, m_l, norm_g[l, 0], w_ffn_in[0], w_ffn_out[0], sub=0, n_tiles=N_ALL_TILES)
        q, k, v, f, ga, gf = _proj(xs, m_l, norm_g[l, 1], w_in[l].astype(BF16), tabs, wc, b_gate[l])
        attn = _attention(q, k, v, lam_params[l], subln_g[l], lam_init=lam_init, with_ctx_queries=not last)
        four = _fourier_latent(f, t1, w2)
        if not last:
            four = jnp.concatenate([four, _fourier_ctx(f, wctx)], axis=0)
        n_tiles = N_LAT_TILES if last else N_ALL_TILES
        xs = _merge(xs, m_l, attn, four, ga, gf, w_attn_out[l].astype(BF16), w_four_out[l].astype(BF16),
                    w_o[l].astype(BF16), n_tiles=n_tiles)
        xs = _ffn(xs, m_l, norm_g[l, 2], w_ffn_in[1], w_ffn_out[1], sub=2, n_tiles=n_tiles,
                  final_g=final_g if last else None)
    return xs.reshape(BATCH, SEQ, D_MODEL)
```

```python
import functools
import math

import jax
import jax.numpy as jnp
from jax import lax
from jax.experimental import pallas as pl
from jax.experimental.pallas import tpu as pltpu

D_MODEL = 1024
BATCH = 2
SEQ = 8192
DEPTH = 2
CTX_LEN = 256
GRID_W = 64
N_HEADS = 8
HEAD_DIM = 64
V_DIM = 2 * HEAD_DIM
QK_W = N_HEADS * 2 * HEAD_DIM
ATTN_WIDTH = N_HEADS * V_DIM
N_FGROUPS = 4
FGROUP_DIM = 128
FOURIER_WIDTH = N_FGROUPS * FGROUP_DIM
F_OFF = 2 * QK_W + ATTN_WIDTH
G_OFF = F_OFF + FOURIER_WIDTH
IN_WIDTH = G_OFF + 2 * D_MODEL
D_FF = 2816
AXIS_DIM = HEAD_DIM // 2
ROPE_BASE = 10000.0
N_MOD = 9
EPS = 1e-6

R_LAT = BATCH * SEQ
R_ALL = R_LAT + BATCH * CTX_LEN
TM = 512
N_LAT_TILES = R_LAT // TM
N_ALL_TILES = R_ALL // TM
FF_CHUNK = 256
N_FF_CHUNKS = D_FF // FF_CHUNK
TQ = 256
TK = 768
KT = 256
N_KEYS = CTX_LEN + SEQ
N_KEY_BLOCKS = N_KEYS // TK
N_LAT_QT = SEQ // TQ
DFT_A = 64
DFT_B = 128
NS2 = 8
NK1 = 8
VMEM_LIMIT = 56 * 1024 * 1024

F32 = jnp.float32
BF16 = jnp.bfloat16


def _dot(a, b):
    return jnp.dot(a, b, preferred_element_type=F32)


def _sigmoid(x):
    return 1.0 / (1.0 + jnp.exp(-x))


def _mod_row(i):
    return jnp.minimum(i // (SEQ // TM), 2)


def _norm_mod(x, g, shift, scale):
    var = jnp.mean(x * x, axis=-1, keepdims=True)
    y = x * lax.rsqrt(var + EPS) * g
    return y * (1.0 + scale) + shift


def _mod_kernel(c_ref, w_ref, b_ref, o_ref):
    c = c_ref[...]
    s = c * _sigmoid(c)
    o_ref[0] = jnp.dot(s, w_ref[0], preferred_element_type=F32,
                       precision=lax.Precision.HIGHEST) + b_ref[0]


def _modulation(conds, w_mod, b_mod):
    tn = D_MODEL
    out = pl.pallas_call(
        _mod_kernel,
        grid=(DEPTH, N_MOD),
        in_specs=[
            pl.BlockSpec((8, D_MODEL), lambda l, j: (0, 0)),
            pl.BlockSpec((1, D_MODEL, tn), lambda l, j: (l, 0, j)),
            pl.BlockSpec((1, 1, tn), lambda l, j: (l, 0, j)),
        ],
        out_specs=pl.BlockSpec((1, 8, tn), lambda l, j: (l, 0, j)),
        out_shape=jax.ShapeDtypeStruct((DEPTH, 8, N_MOD * D_MODEL), F32),
        compiler_params=pltpu.CompilerParams(
            dimension_semantics=("arbitrary", "arbitrary"), vmem_limit_bytes=VMEM_LIMIT),
        name="modulation",
    )(conds, w_mod, b_mod.reshape(DEPTH, 1, N_MOD * D_MODEL))
    return out.reshape(DEPTH, 8, N_MOD, D_MODEL)


def _ffn_kernel(*refs, sub, final):
    if final:
        x_ref, m_ref, g_ref, win_ref, wout_ref, fg_ref, o_ref, u_ref, acc_ref = refs
    else:
        x_ref, m_ref, g_ref, win_ref, wout_ref, o_ref, u_ref, acc_ref = refs
    x = x_ref[...]
    m = m_ref[0, 0]
    shift = m[3 * sub:3 * sub + 1]
    scale = m[3 * sub + 1:3 * sub + 2]
    gate = m[3 * sub + 2:3 * sub + 3]
    u_ref[...] = _norm_mod(x, g_ref[...], shift, scale).astype(BF16)
    acc_ref[...] = jnp.zeros_like(acc_ref)

    def body(c, carry):
        u = u_ref[...]
        a = _dot(u, win_ref[c])
        b = _dot(u, win_ref[c + N_FF_CHUNKS])
        h = (a * _sigmoid(a) * b).astype(BF16)
        acc_ref[...] += _dot(h, wout_ref[c])
        return carry

    lax.fori_loop(0, N_FF_CHUNKS, body, 0)
    y = x + 0.5 * gate * acc_ref[...]
    if final:
        var = jnp.mean(y * y, axis=-1, keepdims=True)
        y = y * lax.rsqrt(var + EPS) * fg_ref[...]
    o_ref[...] = y


def _ffn(x, m_l, g, w_in_c, w_out_c, *, sub, n_tiles, final_g=None):
    final = final_g is not None
    const = dict(pipeline_mode=pl.Buffered(1))
    in_specs = [
        pl.BlockSpec((TM, D_MODEL), lambda i: (i, 0)),
        pl.BlockSpec((1, 1, N_MOD, D_MODEL), lambda i: (0, _mod_row(i), 0, 0)),
        pl.BlockSpec((1, D_MODEL), lambda i: (0, 0)),
        pl.BlockSpec((2 * N_FF_CHUNKS, D_MODEL, FF_CHUNK), lambda i: (0, 0, 0), **const),
        pl.BlockSpec((N_FF_CHUNKS, FF_CHUNK, D_MODEL), lambda i: (0, 0, 0), **const),
    ]
    args = [x, m_l, g.reshape(1, D_MODEL), w_in_c, w_out_c]
    if final:
        in_specs.append(pl.BlockSpec((1, D_MODEL), lambda i: (0, 0)))
        args.append(final_g.reshape(1, D_MODEL))
    return pl.pallas_call(
        functools.partial(_ffn_kernel, sub=sub, final=final),
        grid=(n_tiles,),
        in_specs=in_specs,
        out_specs=pl.BlockSpec((TM, D_MODEL), lambda i: (i, 0)),
        out_shape=jax.ShapeDtypeStruct((n_tiles * TM, D_MODEL), F32),
        scratch_shapes=[pltpu.VMEM((TM, D_MODEL), BF16), pltpu.VMEM((TM, D_MODEL), F32)],
        compiler_params=pltpu.CompilerParams(
            dimension_semantics=("arbitrary",), vmem_limit_bytes=VMEM_LIMIT),
        name="ffn",
    )(*args)


def _proj_kernel(x_ref, m_ref, g_ref, w_ref, wvt_ref, cos_ref, sa_ref, sb_ref, wc_ref, bg_ref,
                 q_ref, k_ref, vt_ref, f_ref, ga_ref, gf_ref, u_ref):
    x = x_ref[...]
    m = m_ref[0, 0]
    u_ref[...] = _norm_mod(x, g_ref[...], m[3:4], m[4:5]).astype(BF16)
    cw = 256
    cos = jnp.concatenate([cos_ref[...]] * (cw // 128), axis=1)
    sa = jnp.concatenate([sa_ref[...]] * (cw // 128), axis=1)
    sb = jnp.concatenate([sb_ref[...]] * (cw // 128), axis=1)

    def col(off):
        return _dot(u_ref[...], w_ref[:, off:off + cw])

    for out_ref, base, qscale in ((q_ref, 0, HEAD_DIM ** -0.5 * math.log2(math.e)), (k_ref, QK_W, 1.0)):
        for j in range(QK_W // cw):
            p = col(base + j * cw)
            r = p * cos + pltpu.roll(p, cw - 16, 1) * sa + pltpu.roll(p, 16, 1) * sb
            if qscale != 1.0:
                r = r * qscale
            out_ref[:, j * cw:(j + 1) * cw] = r.astype(BF16)
    for j in range(ATTN_WIDTH // cw):
        vt = lax.dot_general(wvt_ref[j * cw:(j + 1) * cw, :], u_ref[...], (((1,), (1,)), ((), ())),
                             preferred_element_type=F32)
        vt_ref[j * cw:(j + 1) * cw, :] = vt.astype(BF16)
    for j in range(FOURIER_WIDTH // cw):
        p = col(F_OFF + j * cw).astype(BF16)
        for t in range(cw // FGROUP_DIM):
            grp = j * (cw // FGROUP_DIM) + t
            r = _dot(p[:, t * FGROUP_DIM:(t + 1) * FGROUP_DIM], wc_ref[...])
            f_ref[:, grp * FGROUP_DIM:(grp + 1) * FGROUP_DIM] = r[:, :FGROUP_DIM].astype(BF16)
            f_ref[:, FOURIER_WIDTH + grp * FGROUP_DIM:FOURIER_WIDTH + (grp + 1) * FGROUP_DIM] = (
                r[:, FGROUP_DIM:].astype(BF16))
    for out_ref, base, brow in ((ga_ref, G_OFF, 0), (gf_ref, G_OFF + D_MODEL, 1)):
        for j in range(D_MODEL // cw):
            p = col(base + j * cw) + bg_ref[brow:brow + 1, j * cw:(j + 1) * cw]
            out_ref[:, j * cw:(j + 1) * cw] = _sigmoid(p).astype(BF16)


def _proj(x, m_l, g, w_in_b, w_v_t, tabs, wc, b_gate):
    cos, sa, sb = tabs
    const = dict(pipeline_mode=pl.Buffered(1))
    row = lambda i: (i, 0)
    fix = lambda i: (0, 0)
    wide = jax.ShapeDtypeStruct((R_ALL, D_MODEL), BF16)
    row_spec = pl.BlockSpec((TM, D_MODEL), row)
    return pl.pallas_call(
        _proj_kernel,
        grid=(N_ALL_TILES,),
        in_specs=[
            pl.BlockSpec((TM, D_MODEL), row),
            pl.BlockSpec((1, 1, N_MOD, D_MODEL), lambda i: (0, _mod_row(i), 0, 0)),
            pl.BlockSpec((1, D_MODEL), fix),
            pl.BlockSpec((D_MODEL, IN_WIDTH), fix, **const),
            pl.BlockSpec((ATTN_WIDTH, D_MODEL), fix, **const),
            pl.BlockSpec((TM, 128), row),
            pl.BlockSpec((TM, 128), row),
            pl.BlockSpec((TM, 128), row),
            pl.BlockSpec((FGROUP_DIM, 2 * FGROUP_DIM), fix),
            pl.BlockSpec((2, D_MODEL), fix),
        ],
        out_specs=[row_spec, row_spec, pl.BlockSpec((ATTN_WIDTH, TM), lambda i: (0, i)),
                   row_spec, row_spec, row_spec],
        out_shape=[wide, wide, jax.ShapeDtypeStruct((ATTN_WIDTH, R_ALL), BF16), wide, wide, wide],
        scratch_shapes=[pltpu.VMEM((TM, D_MODEL), BF16)],
        compiler_params=pltpu.CompilerParams(
            dimension_semantics=("arbitrary",), vmem_limit_bytes=VMEM_LIMIT),
        name="proj",
    )(x, m_l, g.reshape(1, D_MODEL), w_in_b, w_v_t, cos, sa, sb, wc, b_gate)


def _attn_kernel(lp_ref, sg_ref, q_ref, kc_ref, vc_ref, kl_ref, vl_ref, o_ref,
                 kbuf, vbuf, qs_ref, sa_ref, sb_ref, pa_ref, pb_ref, m_ref, l_ref, acc_ref,
                 *, lam_init, with_ctx_queries):
    qt = pl.program_id(2)

    @pl.when(qt == 0)
    def _():
        kbuf[0:CTX_LEN, :] = kc_ref[...]
        kbuf[CTX_LEN:, :] = kl_ref[...]
        vbuf[0, :, 0:CTX_LEN] = vc_ref[...]
        vbuf[0, :, CTX_LEN:] = vl_ref[:, 0:TK - CTX_LEN]
        for blk in range(1, N_KEY_BLOCKS):
            vbuf[blk] = vl_ref[:, blk * TK - CTX_LEN:(blk + 1) * TK - CTX_LEN]

    q = q_ref[...]
    lane = lax.broadcasted_iota(jnp.int32, q.shape, 1)
    zero = jnp.zeros_like(q)
    qs_ref[0:TQ, :] = jnp.where(lane < HEAD_DIM, q, zero)
    qs_ref[TQ:, :] = jnp.where(lane >= HEAD_DIM, q, zero)

    def put_scores(s_ref, pm_ref, keys, n):
        s = lax.dot_general(keys, qs_ref[...], (((1,), (1,)), ((), ())), preferred_element_type=F32)
        s_ref[0:n, :] = s
        pm_ref[...] = jnp.max(s.reshape(n // 8, 8, 2 * TQ), axis=0)

    def init():
        m_ref[...] = jnp.full_like(m_ref, -jnp.inf)
        l_ref[...] = jnp.zeros_like(l_ref)
        acc_ref[...] = jnp.zeros_like(acc_ref)

    def softmax_pv(s_ref, pm_ref, vt, n):
        m_old = m_ref[...]
        m_new = jnp.maximum(m_old, jnp.max(pm_ref[...], axis=0, keepdims=True))
        alpha = jnp.exp2(m_old - m_new)
        p = jnp.exp2(s_ref[0:n, :] - m_new)
        m_ref[...] = m_new
        l_ref[...] = alpha * l_ref[...] + jnp.sum(p.reshape(n // 8, 8, 2 * TQ), axis=0)
        acc_ref[...] = alpha * acc_ref[...] + _dot(vt, p.astype(BF16))

    def finish():
        lp = lp_ref[...]
        lam = (jnp.exp(jnp.sum(lp[0:1] * lp[1:2], axis=-1, keepdims=True))
               - jnp.exp(jnp.sum(lp[2:3] * lp[3:4], axis=-1, keepdims=True)) + lam_init)
        o = acc_ref[...] / jnp.sum(l_ref[...], axis=0, keepdims=True)
        d = o[:, :TQ] - lam * o[:, TQ:]
        var = jnp.mean(d * d, axis=0, keepdims=True)
        y = d * lax.rsqrt(var + EPS) * sg_ref[...] * (1.0 - lam_init)
        o_ref[...] = y.T.astype(BF16)

    def key_block(blk):
        return kbuf[pl.ds(pl.multiple_of(blk * TK, TK), TK), :]

    def latent_queries():
        init()
        put_scores(sa_ref, pa_ref, key_block(0), TK)

        def body(j, carry):
            put_scores(sb_ref, pb_ref, key_block(2 * j + 1), TK)
            softmax_pv(sa_ref, pa_ref, vbuf[2 * j], TK)
            put_scores(sa_ref, pa_ref, key_block(2 * j + 2), TK)
            softmax_pv(sb_ref, pb_ref, vbuf[2 * j + 1], TK)
            return carry

        lax.fori_loop(0, (N_KEY_BLOCKS - 1) // 2, body, 0)
        softmax_pv(sa_ref, pa_ref, vbuf[N_KEY_BLOCKS - 1], TK)
        finish()

    def ctx_queries():
        init()
        put_scores(sa_ref, pa_ref, kc_ref[...], CTX_LEN)
        softmax_pv(sa_ref, pa_ref, vc_ref[...], CTX_LEN)
        finish()

    if with_ctx_queries:
        pl.when(qt < N_LAT_QT)(latent_queries)
        pl.when(qt == N_LAT_QT)(ctx_queries)
    else:
        latent_queries()


def _attention(q, k, vt, lam_params_l, subln_g_l, *, lam_init, with_ctx_queries):
    n_qt = N_LAT_QT + (1 if with_ctx_queries else 0)
    ctx_blk0 = R_LAT // CTX_LEN

    def q_map(b, h, t):
        return (jnp.where(t < N_LAT_QT, b * N_LAT_QT + t, ctx_blk0 + b), h)

    rows = R_ALL if with_ctx_queries else R_LAT
    return pl.pallas_call(
        functools.partial(_attn_kernel, lam_init=lam_init, with_ctx_queries=with_ctx_queries),
        grid=(BATCH, N_HEADS, n_qt),
        in_specs=[
            pl.BlockSpec((4, HEAD_DIM), lambda b, h, t: (0, 0)),
            pl.BlockSpec((V_DIM, 1), lambda b, h, t: (0, 0)),
            pl.BlockSpec((TQ, V_DIM), q_map),
            pl.BlockSpec((CTX_LEN, V_DIM), lambda b, h, t: (ctx_blk0 + b, h)),
            pl.BlockSpec((V_DIM, CTX_LEN), lambda b, h, t: (h, ctx_blk0 + b)),
            pl.BlockSpec((SEQ, V_DIM), lambda b, h, t: (b, h)),
            pl.BlockSpec((V_DIM, SEQ), lambda b, h, t: (h, b)),
        ],
        out_specs=pl.BlockSpec((TQ, V_DIM), q_map),
        out_shape=jax.ShapeDtypeStruct((rows, ATTN_WIDTH), BF16),
        scratch_shapes=[
            pltpu.VMEM((N_KEYS, V_DIM), BF16),
            pltpu.VMEM((N_KEY_BLOCKS, V_DIM, TK), BF16),
            pltpu.VMEM((2 * TQ, V_DIM), BF16),
            pltpu.VMEM((TK, 2 * TQ), F32),
            pltpu.VMEM((TK, 2 * TQ), F32),
            pltpu.VMEM((8, 2 * TQ), F32),
            pltpu.VMEM((8, 2 * TQ), F32),
            pltpu.VMEM((1, 2 * TQ), F32),
            pltpu.VMEM((8, 2 * TQ), F32),
            pltpu.VMEM((V_DIM, 2 * TQ), F32),
        ],
        compiler_params=pltpu.CompilerParams(
            dimension_semantics=("arbitrary", "arbitrary", "arbitrary"), vmem_limit_bytes=VMEM_LIMIT),
        name="diff_attn",
    )(lam_params_l, subln_g_l.reshape(V_DIM, 1), q, k, vt, k, vt)


def _dft1_kernel(x_ref, t_ref, o_ref):
    w = 2 * FOURIER_WIDTH
    for j in range(NS2):
        xs = jnp.concatenate([x_ref[:, j * w:j * w + FOURIER_WIDTH],
                              x_ref[:, j * w + FOURIER_WIDTH:(j + 1) * w]], axis=0)
        r = _dot(t_ref[j], xs)
        o_ref[:, j * w:j * w + FOURIER_WIDTH] = r[:DFT_A].astype(BF16)
        o_ref[:, j * w + FOURIER_WIDTH:(j + 1) * w] = r[DFT_A:].astype(BF16)


def _dft2_kernel(x_ref, w_ref, o_ref, *, n, nk, scale):
    for j in range(nk):
        xs = jnp.concatenate([x_ref[j * n:(j + 1) * n, :FOURIER_WIDTH],
                              x_ref[j * n:(j + 1) * n, FOURIER_WIDTH:]], axis=0)
        r = _dot(w_ref[...], xs) * scale
        o_ref[:, j * FOURIER_WIDTH:(j + 1) * FOURIER_WIDTH] = r.astype(BF16)


def _fourier_latent(f_all, t1, w2):
    width = DFT_B * 2 * FOURIER_WIDTH
    x2 = pl.pallas_call(
        _dft1_kernel,
        grid=(BATCH, DFT_B // NS2),
        in_specs=[
            pl.BlockSpec((DFT_A, NS2 * 2 * FOURIER_WIDTH), lambda b, j: (b, j)),
            pl.BlockSpec((NS2, 2 * DFT_A, 2 * DFT_A), lambda b, j: (j, 0, 0)),
        ],
        out_specs=pl.BlockSpec((DFT_A, NS2 * 2 * FOURIER_WIDTH), lambda b, j: (b, j)),
        out_shape=jax.ShapeDtypeStruct((BATCH * DFT_A, width), BF16),
        compiler_params=pltpu.CompilerParams(
            dimension_semantics=("arbitrary", "arbitrary"), vmem_limit_bytes=VMEM_LIMIT),
        name="dft_stage1",
    )(f_all.reshape(R_ALL // DFT_B, width), t1)
    y = pl.pallas_call(
        functools.partial(_dft2_kernel, n=DFT_B, nk=NK1, scale=(SEQ * FGROUP_DIM) ** -0.5),
        grid=(BATCH, DFT_A // NK1),
        in_specs=[
            pl.BlockSpec((NK1 * DFT_B, 2 * FOURIER_WIDTH), lambda b, j: (b * (DFT_A // NK1) + j, 0)),
            pl.BlockSpec((DFT_B, 2 * DFT_B), lambda b, j: (0, 0)),
        ],
        out_specs=pl.BlockSpec((DFT_B, NK1 * FOURIER_WIDTH), lambda b, j: (b, j)),
        out_shape=jax.ShapeDtypeStruct((BATCH * DFT_B, DFT_A * FOURIER_WIDTH), BF16),
        compiler_params=pltpu.CompilerParams(
            dimension_semantics=("arbitrary", "arbitrary"), vmem_limit_bytes=VMEM_LIMIT),
        name="dft_stage2",
    )(x2.reshape(R_LAT, 2 * FOURIER_WIDTH), w2)
    return y.reshape(R_LAT, FOURIER_WIDTH)


def _fourier_ctx(f_all, wctx):
    ctx_blk0 = R_LAT // CTX_LEN
    return pl.pallas_call(
        functools.partial(_dft2_kernel, n=CTX_LEN, nk=1, scale=(CTX_LEN * FGROUP_DIM) ** -0.5),
        grid=(BATCH,),
        in_specs=[
            pl.BlockSpec((CTX_LEN, 2 * FOURIER_WIDTH), lambda b: (ctx_blk0 + b, 0)),
            pl.BlockSpec((CTX_LEN, 2 * CTX_LEN), lambda b: (0, 0)),
        ],
        out_specs=pl.BlockSpec((CTX_LEN, FOURIER_WIDTH), lambda b: (b, 0)),
        out_shape=jax.ShapeDtypeStruct((BATCH * CTX_LEN, FOURIER_WIDTH), BF16),
        compiler_params=pltpu.CompilerParams(
            dimension_semantics=("arbitrary",), vmem_limit_bytes=VMEM_LIMIT),
        name="dft_ctx",
    )(f_all, wctx)


def _merge_kernel(x_ref, m_ref, a_ref, f_ref, ga_ref, gf_ref, wa_ref, wf_ref, wo_ref, o_ref):
    a = _dot(a_ref[...], wa_ref[...])
    fr = _dot(f_ref[...], wf_ref[...])
    merged = ga_ref[...].astype(F32) * a + gf_ref[...].astype(F32) * fr
    mix = _dot(merged.astype(BF16), wo_ref[...])
    o_ref[...] = x_ref[...] + m_ref[0, 0][5:6] * mix


def _merge(x, m_l, attn, four, ga, gf, wa, wf, wo, *, n_tiles):
    const = dict(pipeline_mode=pl.Buffered(1))
    row = lambda i: (i, 0)
    fix = lambda i: (0, 0)
    return pl.pallas_call(
        _merge_kernel,
        grid=(n_tiles,),
        in_specs=[
            pl.BlockSpec((TM, D_MODEL), row),
            pl.BlockSpec((1, 1, N_MOD, D_MODEL), lambda i: (0, _mod_row(i), 0, 0)),
            pl.BlockSpec((TM, ATTN_WIDTH), row),
            pl.BlockSpec((TM, FOURIER_WIDTH), row),
            pl.BlockSpec((TM, D_MODEL), row),
            pl.BlockSpec((TM, D_MODEL), row),
            pl.BlockSpec((ATTN_WIDTH, D_MODEL), fix, **const),
            pl.BlockSpec((FOURIER_WIDTH, D_MODEL), fix, **const),
            pl.BlockSpec((D_MODEL, D_MODEL), fix, **const),
        ],
        out_specs=pl.BlockSpec((TM, D_MODEL), row),
        out_shape=jax.ShapeDtypeStruct((n_tiles * TM, D_MODEL), F32),
        compiler_params=pltpu.CompilerParams(
            dimension_semantics=("arbitrary",), vmem_limit_bytes=VMEM_LIMIT),
        name="merge",
    )(x, m_l, attn, four, ga, gf, wa, wf, wo)


def _rope_tables():
    s = jnp.arange(SEQ, dtype=jnp.int32)
    lane = jnp.arange(128, dtype=jnp.int32) % HEAD_DIM
    inv = 1.0 / (ROPE_BASE ** ((2 * (lane % (AXIS_DIM // 2))).astype(F32) / AXIS_DIM))
    pos = jnp.where(lane[None, :] < AXIS_DIM, (s // GRID_W)[:, None], (s % GRID_W)[:, None]).astype(F32)
    ang = pos * inv[None, :]
    cos, sin = jnp.cos(ang), jnp.sin(ang)
    first = (lane % AXIS_DIM) < (AXIS_DIM // 2)
    sa = jnp.where(first[None, :], -sin, 0.0)
    sb = jnp.where(first[None, :], 0.0, sin)
    n_ctx = BATCH * CTX_LEN

    def full(t, fill):
        return jnp.concatenate([t] * BATCH + [jnp.full((n_ctx, 128), fill, F32)], axis=0)

    return full(cos, 1.0), full(sa, 0.0), full(sb, 0.0)


def _angles(num, den):
    return (2.0 * math.pi / den) * (num % den).astype(F32)


def _dft_tables():
    c = jnp.arange(FGROUP_DIM, dtype=jnp.int32)
    ang = _angles(c[:, None] * c[None, :], FGROUP_DIM)
    wc = jnp.concatenate([jnp.cos(ang), -jnp.sin(ang)], axis=1).astype(BF16)

    s2 = jnp.arange(DFT_B, dtype=jnp.int32)[:, None, None]
    k1 = jnp.arange(DFT_A, dtype=jnp.int32)[None, :, None]
    s1 = jnp.arange(DFT_A, dtype=jnp.int32)[None, None, :]
    ang = _angles(k1 * (DFT_B * s1 + s2), SEQ)
    cs, sn = jnp.cos(ang), jnp.sin(ang)
    t1 = jnp.concatenate([jnp.concatenate([cs, sn], axis=2),
                          jnp.concatenate([-sn, cs], axis=2)], axis=1).astype(BF16)

    b = jnp.arange(DFT_B, dtype=jnp.int32)
    ang = _angles(b[:, None] * b[None, :], DFT_B)
    w2 = jnp.concatenate([jnp.cos(ang), jnp.sin(ang)], axis=1).astype(BF16)

    n = jnp.arange(CTX_LEN, dtype=jnp.int32)
    ang = _angles(n[:, None] * n[None, :], CTX_LEN)
    wctx = jnp.concatenate([jnp.cos(ang), jnp.sin(ang)], axis=1).astype(BF16)
    return wc, t1, w2, wctx


def kernel(x, c, ctx, c_ctx, w_mod, b_mod, norm_g, ffn_w_in, ffn_w_out, w_in, b_gate,
           lam_params, subln_g, w_attn_out, w_four_out, w_o, final_g):
    xs = jnp.concatenate([x.reshape(R_LAT, D_MODEL), ctx.reshape(BATCH * CTX_LEN, D_MODEL)], axis=0)
    conds = jnp.concatenate([c, c_ctx[None, :], jnp.zeros((8 - BATCH - 1, D_MODEL), F32)], axis=0)
    m_all = _modulation(conds, w_mod, b_mod)
    tabs = _rope_tables()
    wc, t1, w2, wctx = _dft_tables()

    for l in range(DEPTH):
        last = l == DEPTH - 1
        lam_init = 0.8 - 0.6 * math.exp(-0.3 * l)
        m_l = m_all[l:l + 1]
        w_ffn_in = [ffn_w_in[l, i].astype(BF16).reshape(D_MODEL, 2 * N_FF_CHUNKS, FF_CHUNK).transpose(1, 0, 2)
                    for i in range(2)]
        w_ffn_out = [ffn_w_out[l, i].astype(BF16).reshape(N_FF_CHUNKS, FF_CHUNK, D_MODEL) for i in range(2)]

        xs = _ffn(xs, m_l, norm_g[l, 0], w_ffn_in[0], w_ffn_out[0], sub=0, n_tiles=N_ALL_TILES)
        w_v_t = w_in[l, :, 2 * QK_W:F_OFF].T.astype(BF16)
        q, k, vt, f, ga, gf = _proj(xs, m_l, norm_g[l, 1], w_in[l].astype(BF16), w_v_t, tabs, wc, b_gate[l])
        attn = _attention(q, k, vt, lam_params[l], subln_g[l], lam_init=lam_init, with_ctx_queries=not last)
        four = _fourier_latent(f, t1, w2)
        if not last:
            four = jnp.concatenate([four, _fourier_ctx(f, wctx)], axis=0)
        n_tiles = N_LAT_TILES if last else N_ALL_TILES
        xs = _merge(xs, m_l, attn, four, ga, gf, w_attn_out[l].astype(BF16), w_four_out[l].astype(BF16),
                    w_o[l].astype(BF16), n_tiles=n_tiles)
        xs = _ffn(xs, m_l, norm_g[l, 2], w_ffn_in[1], w_ffn_out[1], sub=2, n_tiles=n_tiles,
                  final_g=final_g if last else None)
    return xs.reshape(BATCH, SEQ, D_MODEL)
```

```python
import functools
import math

import jax
import jax.numpy as jnp
from jax import lax
from jax.experimental import pallas as pl
from jax.experimental.pallas import tpu as pltpu

D_MODEL = 1024
BATCH = 2
SEQ = 8192
DEPTH = 2
CTX_LEN = 256
GRID_W = 64
N_HEADS = 8
HEAD_DIM = 64
V_DIM = 2 * HEAD_DIM
QK_W = N_HEADS * 2 * HEAD_DIM
ATTN_WIDTH = N_HEADS * V_DIM
N_FGROUPS = 4
FGROUP_DIM = 128
FOURIER_WIDTH = N_FGROUPS * FGROUP_DIM
F_OFF = 2 * QK_W + ATTN_WIDTH
G_OFF = F_OFF + FOURIER_WIDTH
IN_WIDTH = G_OFF + 2 * D_MODEL
D_FF = 2816
AXIS_DIM = HEAD_DIM // 2
ROPE_BASE = 10000.0
N_MOD = 9
EPS = 1e-6

R_LAT = BATCH * SEQ
R_ALL = R_LAT + BATCH * CTX_LEN
TM = 512
N_LAT_TILES = R_LAT // TM
N_ALL_TILES = R_ALL // TM
FF_CHUNK = 256
N_FF_CHUNKS = D_FF // FF_CHUNK
TQ = 256
TK = 1408
N_KEYS = CTX_LEN + SEQ
N_KEY_BLOCKS = N_KEYS // TK
N_LAT_QT = SEQ // TQ
DFT_A = 64
DFT_B = 128
NS2 = 8
NK1 = 8
VMEM_LIMIT = 56 * 1024 * 1024

F32 = jnp.float32
BF16 = jnp.bfloat16


def _dot(a, b):
    return jnp.dot(a, b, preferred_element_type=F32)


def _sigmoid(x):
    return 1.0 / (1.0 + jnp.exp(-x))


def _mod_row(i):
    return jnp.minimum(i // (SEQ // TM), 2)


def _norm_mod(x, g, shift, scale):
    var = jnp.mean(x * x, axis=-1, keepdims=True)
    y = x * lax.rsqrt(var + EPS) * g
    return y * (1.0 + scale) + shift


def _mod_kernel(c_ref, w_ref, b_ref, o_ref):
    c = c_ref[...]
    s = c * _sigmoid(c)
    o_ref[0] = jnp.dot(s, w_ref[0], preferred_element_type=F32,
                       precision=lax.Precision.HIGHEST) + b_ref[0]


def _modulation(conds, w_mod, b_mod):
    tn = D_MODEL
    out = pl.pallas_call(
        _mod_kernel,
        grid=(DEPTH, N_MOD),
        in_specs=[
            pl.BlockSpec((8, D_MODEL), lambda l, j: (0, 0)),
            pl.BlockSpec((1, D_MODEL, tn), lambda l, j: (l, 0, j)),
            pl.BlockSpec((1, 1, tn), lambda l, j: (l, 0, j)),
        ],
        out_specs=pl.BlockSpec((1, 8, tn), lambda l, j: (l, 0, j)),
        out_shape=jax.ShapeDtypeStruct((DEPTH, 8, N_MOD * D_MODEL), F32),
        compiler_params=pltpu.CompilerParams(
            dimension_semantics=("arbitrary", "arbitrary"), vmem_limit_bytes=VMEM_LIMIT),
        name="modulation",
    )(conds, w_mod, b_mod.reshape(DEPTH, 1, N_MOD * D_MODEL))
    return out.reshape(DEPTH, 8, N_MOD, D_MODEL)


def _ffn_kernel(*refs, sub, final, split_input):
    refs = list(refs)
    x_ref = refs.pop(0)
    xc_ref = refs.pop(0) if split_input else None
    m_ref, g_ref, win_ref, wout_ref = refs[:4]
    fg_ref = refs[4] if final else None
    o_ref, u_ref, h_ref = refs[-3:]
    x = x_ref[...]
    if split_input:
        x = jnp.where(pl.program_id(0) < N_LAT_TILES, x, xc_ref[...])
    m = m_ref[0, 0]
    shift = m[3 * sub:3 * sub + 1]
    scale = m[3 * sub + 1:3 * sub + 2]
    gate = m[3 * sub + 2:3 * sub + 3]
    u_ref[...] = _norm_mod(x, g_ref[...], shift, scale).astype(BF16)
    for c in range(N_FF_CHUNKS):
        lo = c * FF_CHUNK
        a = _dot(u_ref[...], win_ref[:, lo:lo + FF_CHUNK])
        b = _dot(u_ref[...], win_ref[:, D_FF + lo:D_FF + lo + FF_CHUNK])
        h_ref[:, lo:lo + FF_CHUNK] = (a * _sigmoid(a) * b).astype(BF16)
    y = x + 0.5 * gate * _dot(h_ref[...], wout_ref[...])
    if final:
        var = jnp.mean(y * y, axis=-1, keepdims=True)
        y = y * lax.rsqrt(var + EPS) * fg_ref[...]
    o_ref[...] = y


def _ffn(x, m_l, g, w_in_c, w_out_c, *, sub, n_tiles, final_g=None, x_ctx=None):
    final = final_g is not None
    split_input = x_ctx is not None
    const = dict(pipeline_mode=pl.Buffered(1))
    if split_input:
        in_specs = [pl.BlockSpec((TM, D_MODEL), lambda i: (jnp.minimum(i, N_LAT_TILES - 1), 0)),
                    pl.BlockSpec((TM, D_MODEL), lambda i: (0, 0))]
        args = [x, x_ctx]
    else:
        in_specs = [pl.BlockSpec((TM, D_MODEL), lambda i: (i, 0))]
        args = [x]
    in_specs += [
        pl.BlockSpec((1, 1, N_MOD, D_MODEL), lambda i: (0, _mod_row(i), 0, 0)),
        pl.BlockSpec((1, D_MODEL), lambda i: (0, 0)),
        pl.BlockSpec((D_MODEL, 2 * D_FF), lambda i: (0, 0), **const),
        pl.BlockSpec((D_FF, D_MODEL), lambda i: (0, 0), **const),
    ]
    args += [m_l, g.reshape(1, D_MODEL), w_in_c, w_out_c]
    if final:
        in_specs.append(pl.BlockSpec((1, D_MODEL), lambda i: (0, 0)))
        args.append(final_g.reshape(1, D_MODEL))
    return pl.pallas_call(
        functools.partial(_ffn_kernel, sub=sub, final=final, split_input=split_input),
        grid=(n_tiles,),
        in_specs=in_specs,
        out_specs=pl.BlockSpec((TM, D_MODEL), lambda i: (i, 0)),
        out_shape=jax.ShapeDtypeStruct((n_tiles * TM, D_MODEL), F32),
        scratch_shapes=[pltpu.VMEM((TM, D_MODEL), BF16), pltpu.VMEM((TM, D_FF), BF16)],
        compiler_params=pltpu.CompilerParams(
            dimension_semantics=("arbitrary",), vmem_limit_bytes=VMEM_LIMIT),
        name="ffn",
    )(*args)


def _proj_kernel(x_ref, m_ref, g_ref, w_ref, wvt_ref, cos_ref, sa_ref, sb_ref, wc_ref, bg_ref,
                 q_ref, k_ref, vt_ref, f_ref, ga_ref, gf_ref, u_ref):
    x = x_ref[...]
    m = m_ref[0, 0]
    u_ref[...] = _norm_mod(x, g_ref[...], m[3:4], m[4:5]).astype(BF16)
    cw = 256
    cos = jnp.concatenate([cos_ref[...]] * (cw // 128), axis=1)
    sa = jnp.concatenate([sa_ref[...]] * (cw // 128), axis=1)
    sb = jnp.concatenate([sb_ref[...]] * (cw // 128), axis=1)

    def col(off):
        return _dot(u_ref[...], w_ref[:, off:off + cw])

    for out_ref, base, qscale in ((q_ref, 0, HEAD_DIM ** -0.5 * math.log2(math.e)), (k_ref, QK_W, 1.0)):
        for j in range(QK_W // cw):
            p = col(base + j * cw)
            r = p * cos + pltpu.roll(p, cw - 16, 1) * sa + pltpu.roll(p, 16, 1) * sb
            if qscale != 1.0:
                r = r * qscale
            out_ref[:, j * cw:(j + 1) * cw] = r.astype(BF16)
    for j in range(ATTN_WIDTH // cw):
        vt = lax.dot_general(wvt_ref[j * cw:(j + 1) * cw, :], u_ref[...], (((1,), (1,)), ((), ())),
                             preferred_element_type=F32)
        vt_ref[j * cw:(j + 1) * cw, :] = vt.astype(BF16)
    for j in range(FOURIER_WIDTH // cw):
        p = col(F_OFF + j * cw).astype(BF16)
        for t in range(cw // FGROUP_DIM):
            grp = j * (cw // FGROUP_DIM) + t
            r = _dot(p[:, t * FGROUP_DIM:(t + 1) * FGROUP_DIM], wc_ref[...])
            f_ref[:, grp * FGROUP_DIM:(grp + 1) * FGROUP_DIM] = r[:, :FGROUP_DIM].astype(BF16)
            f_ref[:, FOURIER_WIDTH + grp * FGROUP_DIM:FOURIER_WIDTH + (grp + 1) * FGROUP_DIM] = (
                r[:, FGROUP_DIM:].astype(BF16))
    for out_ref, base, brow in ((ga_ref, G_OFF, 0), (gf_ref, G_OFF + D_MODEL, 1)):
        for j in range(D_MODEL // cw):
            p = col(base + j * cw) + bg_ref[brow:brow + 1, j * cw:(j + 1) * cw]
            out_ref[:, j * cw:(j + 1) * cw] = _sigmoid(p).astype(BF16)


def _proj(x, m_l, g, w_in_b, w_v_t, tabs, wc, b_gate):
    cos, sa, sb = tabs
    const = dict(pipeline_mode=pl.Buffered(1))
    row = lambda i: (i, 0)
    fix = lambda i: (0, 0)
    wide = jax.ShapeDtypeStruct((R_ALL, D_MODEL), BF16)
    row_spec = pl.BlockSpec((TM, D_MODEL), row)
    return pl.pallas_call(
        _proj_kernel,
        grid=(N_ALL_TILES,),
        in_specs=[
            pl.BlockSpec((TM, D_MODEL), row),
            pl.BlockSpec((1, 1, N_MOD, D_MODEL), lambda i: (0, _mod_row(i), 0, 0)),
            pl.BlockSpec((1, D_MODEL), fix),
            pl.BlockSpec((D_MODEL, IN_WIDTH), fix, **const),
            pl.BlockSpec((ATTN_WIDTH, D_MODEL), fix, **const),
            pl.BlockSpec((TM, 128), row),
            pl.BlockSpec((TM, 128), row),
            pl.BlockSpec((TM, 128), row),
            pl.BlockSpec((FGROUP_DIM, 2 * FGROUP_DIM), fix),
            pl.BlockSpec((2, D_MODEL), fix),
        ],
        out_specs=[row_spec, row_spec, pl.BlockSpec((ATTN_WIDTH, TM), lambda i: (0, i)),
                   row_spec, row_spec, row_spec],
        out_shape=[wide, wide, jax.ShapeDtypeStruct((ATTN_WIDTH, R_ALL), BF16), wide, wide, wide],
        scratch_shapes=[pltpu.VMEM((TM, D_MODEL), BF16)],
        compiler_params=pltpu.CompilerParams(
            dimension_semantics=("arbitrary",), vmem_limit_bytes=VMEM_LIMIT),
        name="proj",
    )(x, m_l, g.reshape(1, D_MODEL), w_in_b, w_v_t, cos, sa, sb, wc, b_gate)


def _masked_queries(q):
    lane = lax.broadcasted_iota(jnp.int32, q.shape, 1)
    zero = jnp.zeros_like(q)
    return jnp.where(lane < HEAD_DIM, q, zero), jnp.where(lane >= HEAD_DIM, q, zero)


def _lambda(lp, lam_init):
    return (jnp.exp(jnp.sum(lp[0:1] * lp[1:2], axis=-1, keepdims=True))
            - jnp.exp(jnp.sum(lp[2:3] * lp[3:4], axis=-1, keepdims=True)) + lam_init)


def _attn_epilogue(acc, l, lam, sg, lam_init):
    o = acc / l
    d = o[:, :TQ] - lam * o[:, TQ:]
    var = jnp.mean(d * d, axis=0, keepdims=True)
    y = d * lax.rsqrt(var + EPS) * sg * (1.0 - lam_init)
    return y.T.astype(BF16)


def _attn_kernel(lp_ref, sg_ref, q_ref, kc_ref, vc_ref, kl_ref, vl_ref, o_ref,
                 kbuf, vbuf, qs_ref, sa_ref, sb_ref, pa_ref, pb_ref, ea_ref, eb_ref, m_ref, l_ref, acc_ref,
                 *, lam_init):

    kbuf[0:CTX_LEN, :] = kc_ref[...]
    kbuf[CTX_LEN:, :] = kl_ref[...]
    vbuf[0, :, 0:CTX_LEN] = vc_ref[...]
    vbuf[0, :, CTX_LEN:] = vl_ref[:, 0:TK - CTX_LEN]
    for blk in range(1, N_KEY_BLOCKS):
        vbuf[blk] = vl_ref[:, blk * TK - CTX_LEN:(blk + 1) * TK - CTX_LEN]

    lam = _lambda(lp_ref[...], lam_init)
    bufs = ((sa_ref, pa_ref, ea_ref), (sb_ref, pb_ref, eb_ref))

    def set_queries(t):
        q0, q1 = _masked_queries(q_ref[pl.ds(pl.multiple_of(t * TQ, TQ), TQ), :])
        qs_ref[0:TQ, :] = q0
        qs_ref[TQ:, :] = q1

    def put_scores(buf, blk):
        s_ref, pm_ref, _ = buf
        s = lax.dot_general(kbuf[blk * TK:(blk + 1) * TK, :], qs_ref[...], (((1,), (1,)), ((), ())),
                            preferred_element_type=F32)
        s_ref[...] = s
        pm_ref[...] = jnp.max(s.reshape(TK // 8, 8, 2 * TQ), axis=0)

    def softmax_pv(buf, blk):
        s_ref, pm_ref, p_ref = buf
        m_old = m_ref[...]
        m_new = jnp.maximum(m_old, jnp.max(pm_ref[...], axis=0, keepdims=True))
        alpha = jnp.exp2(m_old - m_new)
        p = jnp.exp2(s_ref[...] - m_new)
        m_ref[...] = m_new
        l_ref[...] = alpha * l_ref[...] + jnp.sum(p.reshape(TK // 8, 8, 2 * TQ), axis=0)
        p_ref[...] = p.astype(BF16)
        acc_ref[...] = alpha * acc_ref[...] + _dot(vbuf[blk], p_ref[...])

    set_queries(0)
    put_scores(bufs[0], 0)

    def tile(t, carry):
        m_ref[...] = jnp.full_like(m_ref, -jnp.inf)
        l_ref[...] = jnp.zeros_like(l_ref)
        acc_ref[...] = jnp.zeros_like(acc_ref)
        for blk in range(N_KEY_BLOCKS):
            nxt = bufs[(blk + 1) % 2]
            if blk + 1 < N_KEY_BLOCKS:
                put_scores(nxt, blk + 1)
            else:
                set_queries(jnp.minimum(t + 1, N_LAT_QT - 1))
                put_scores(nxt, 0)
            softmax_pv(bufs[blk % 2], blk)
        l = jnp.sum(l_ref[...], axis=0, keepdims=True)
        o_ref[pl.ds(pl.multiple_of(t * TQ, TQ), TQ), :] = _attn_epilogue(
            acc_ref[...], l, lam, sg_ref[...], lam_init)
        return carry

    lax.fori_loop(0, N_LAT_QT, tile, 0)


def _attn_ctx_kernel(lp_ref, sg_ref, q_ref, k_ref, vt_ref, buf_ref, o_ref, *, lam_init):
    del buf_ref
    q0, q1 = _masked_queries(q_ref[...])
    qs = jnp.concatenate([q0, q1], axis=0)
    s = lax.dot_general(k_ref[...], qs, (((1,), (1,)), ((), ())), preferred_element_type=F32)
    p = jnp.exp2(s - jnp.max(s, axis=0, keepdims=True))
    l = jnp.sum(p, axis=0, keepdims=True)
    acc = _dot(vt_ref[...], p.astype(BF16))
    o_ref[...] = _attn_epilogue(acc, l, _lambda(lp_ref[...], lam_init), sg_ref[...], lam_init)


def _attention(q, k, vt, lam_params_l, subln_g_l, *, lam_init, with_ctx_queries):
    ctx_blk0 = R_LAT // CTX_LEN
    rows = R_ALL if with_ctx_queries else R_LAT
    sg = subln_g_l.reshape(V_DIM, 1)
    attn = pl.pallas_call(
        functools.partial(_attn_kernel, lam_init=lam_init),
        grid=(BATCH, N_HEADS),
        in_specs=[
            pl.BlockSpec((4, HEAD_DIM), lambda b, h: (0, 0)),
            pl.BlockSpec((V_DIM, 1), lambda b, h: (0, 0)),
            pl.BlockSpec((SEQ, V_DIM), lambda b, h: (b, h)),
            pl.BlockSpec((CTX_LEN, V_DIM), lambda b, h: (ctx_blk0 + b, h)),
            pl.BlockSpec((V_DIM, CTX_LEN), lambda b, h: (h, ctx_blk0 + b)),
            pl.BlockSpec((SEQ, V_DIM), lambda b, h: (b, h)),
            pl.BlockSpec((V_DIM, SEQ), lambda b, h: (h, b)),
        ],
        out_specs=pl.BlockSpec((SEQ, V_DIM), lambda b, h: (b, h)),
        out_shape=jax.ShapeDtypeStruct((rows, ATTN_WIDTH), BF16),
        scratch_shapes=[
            pltpu.VMEM((N_KEYS, V_DIM), BF16),
            pltpu.VMEM((N_KEY_BLOCKS, V_DIM, TK), BF16),
            pltpu.VMEM((2 * TQ, V_DIM), BF16),
            pltpu.VMEM((TK, 2 * TQ), F32),
            pltpu.VMEM((TK, 2 * TQ), F32),
            pltpu.VMEM((8, 2 * TQ), F32),
            pltpu.VMEM((8, 2 * TQ), F32),
            pltpu.VMEM((TK, 2 * TQ), BF16),
            pltpu.VMEM((TK, 2 * TQ), BF16),
            pltpu.VMEM((1, 2 * TQ), F32),
            pltpu.VMEM((8, 2 * TQ), F32),
            pltpu.VMEM((V_DIM, 2 * TQ), F32),
        ],
        compiler_params=pltpu.CompilerParams(
            dimension_semantics=("arbitrary", "arbitrary"), vmem_limit_bytes=VMEM_LIMIT),
        name="diff_attn",
    )(lam_params_l, sg, q, k, vt, k, vt)
    if not with_ctx_queries:
        return attn
    return pl.pallas_call(
        functools.partial(_attn_ctx_kernel, lam_init=lam_init),
        grid=(BATCH, N_HEADS),
        in_specs=[
            pl.BlockSpec((4, HEAD_DIM), lambda b, h: (0, 0)),
            pl.BlockSpec((V_DIM, 1), lambda b, h: (0, 0)),
            pl.BlockSpec((CTX_LEN, V_DIM), lambda b, h: (ctx_blk0 + b, h)),
            pl.BlockSpec((CTX_LEN, V_DIM), lambda b, h: (ctx_blk0 + b, h)),
            pl.BlockSpec((V_DIM, CTX_LEN), lambda b, h: (h, ctx_blk0 + b)),
            pl.BlockSpec(memory_space=pl.ANY),
        ],
        out_specs=pl.BlockSpec((CTX_LEN, V_DIM), lambda b, h: (ctx_blk0 + b, h)),
        out_shape=jax.ShapeDtypeStruct((R_ALL, ATTN_WIDTH), BF16),
        input_output_aliases={5: 0},
        compiler_params=pltpu.CompilerParams(
            dimension_semantics=("arbitrary", "arbitrary"), vmem_limit_bytes=VMEM_LIMIT),
        name="diff_attn_ctx",
    )(lam_params_l, sg, q, k, vt, attn)


def _dft1_kernel(x_ref, t_ref, o_ref):
    w = 2 * FOURIER_WIDTH
    for j in range(NS2):
        xs = jnp.concatenate([x_ref[:, j * w:j * w + FOURIER_WIDTH],
                              x_ref[:, j * w + FOURIER_WIDTH:(j + 1) * w]], axis=0)
        r = _dot(t_ref[j], xs)
        o_ref[:, j * w:j * w + FOURIER_WIDTH] = r[:DFT_A].astype(BF16)
        o_ref[:, j * w + FOURIER_WIDTH:(j + 1) * w] = r[DFT_A:].astype(BF16)


def _dft2_kernel(x_ref, w_ref, o_ref, *, n, nk, scale):
    for j in range(nk):
        xs = jnp.concatenate([x_ref[j * n:(j + 1) * n, :FOURIER_WIDTH],
                              x_ref[j * n:(j + 1) * n, FOURIER_WIDTH:]], axis=0)
        r = _dot(w_ref[...], xs) * scale
        o_ref[:, j * FOURIER_WIDTH:(j + 1) * FOURIER_WIDTH] = r.astype(BF16)


def _fourier_latent(f_all, t1, w2, *, rows):
    width = DFT_B * 2 * FOURIER_WIDTH
    x2 = pl.pallas_call(
        _dft1_kernel,
        grid=(BATCH, DFT_B // NS2),
        in_specs=[
            pl.BlockSpec((DFT_A, NS2 * 2 * FOURIER_WIDTH), lambda b, j: (b, j)),
            pl.BlockSpec((NS2, 2 * DFT_A, 2 * DFT_A), lambda b, j: (j, 0, 0)),
        ],
        out_specs=pl.BlockSpec((DFT_A, NS2 * 2 * FOURIER_WIDTH), lambda b, j: (b, j)),
        out_shape=jax.ShapeDtypeStruct((BATCH * DFT_A, width), BF16),
        compiler_params=pltpu.CompilerParams(
            dimension_semantics=("arbitrary", "arbitrary"), vmem_limit_bytes=VMEM_LIMIT),
        name="dft_stage1",
    )(f_all.reshape(R_ALL // DFT_B, width), t1)
    y = pl.pallas_call(
        functools.partial(_dft2_kernel, n=DFT_B, nk=NK1, scale=(SEQ * FGROUP_DIM) ** -0.5),
        grid=(BATCH, DFT_A // NK1),
        in_specs=[
            pl.BlockSpec((NK1 * DFT_B, 2 * FOURIER_WIDTH), lambda b, j: (b * (DFT_A // NK1) + j, 0)),
            pl.BlockSpec((DFT_B, 2 * DFT_B), lambda b, j: (0, 0)),
        ],
        out_specs=pl.BlockSpec((DFT_B, NK1 * FOURIER_WIDTH), lambda b, j: (b, j)),
        out_shape=jax.ShapeDtypeStruct((rows // DFT_A, DFT_A * FOURIER_WIDTH), BF16),
        compiler_params=pltpu.CompilerParams(
            dimension_semantics=("arbitrary", "arbitrary"), vmem_limit_bytes=VMEM_LIMIT),
        name="dft_stage2",
    )(x2.reshape(R_LAT, 2 * FOURIER_WIDTH), w2)
    return y.reshape(rows, FOURIER_WIDTH)


def _dft_ctx_kernel(x_ref, w_ref, buf_ref, o_ref, **kw):
    del buf_ref
    _dft2_kernel(x_ref, w_ref, o_ref, **kw)


def _fourier_ctx(f_all, wctx, four):
    ctx_blk0 = R_LAT // CTX_LEN
    return pl.pallas_call(
        functools.partial(_dft_ctx_kernel, n=CTX_LEN, nk=1, scale=(CTX_LEN * FGROUP_DIM) ** -0.5),
        grid=(BATCH,),
        in_specs=[
            pl.BlockSpec((CTX_LEN, 2 * FOURIER_WIDTH), lambda b: (ctx_blk0 + b, 0)),
            pl.BlockSpec((CTX_LEN, 2 * CTX_LEN), lambda b: (0, 0)),
            pl.BlockSpec(memory_space=pl.ANY),
        ],
        out_specs=pl.BlockSpec((CTX_LEN, FOURIER_WIDTH), lambda b: (ctx_blk0 + b, 0)),
        out_shape=jax.ShapeDtypeStruct((R_ALL, FOURIER_WIDTH), BF16),
        input_output_aliases={2: 0},
        compiler_params=pltpu.CompilerParams(
            dimension_semantics=("arbitrary",), vmem_limit_bytes=VMEM_LIMIT),
        name="dft_ctx",
    )(f_all, wctx, four)


def _merge_kernel(x_ref, m_ref, a_ref, f_ref, ga_ref, gf_ref, wa_ref, wf_ref, wo_ref, o_ref):
    a = _dot(a_ref[...], wa_ref[...])
    fr = _dot(f_ref[...], wf_ref[...])
    merged = ga_ref[...].astype(F32) * a + gf_ref[...].astype(F32) * fr
    mix = _dot(merged.astype(BF16), wo_ref[...])
    o_ref[...] = x_ref[...] + m_ref[0, 0][5:6] * mix


def _merge(x, m_l, attn, four, ga, gf, wa, wf, wo, *, n_tiles):
    const = dict(pipeline_mode=pl.Buffered(1))
    row = lambda i: (i, 0)
    fix = lambda i: (0, 0)
    return pl.pallas_call(
        _merge_kernel,
        grid=(n_tiles,),
        in_specs=[
            pl.BlockSpec((TM, D_MODEL), row),
            pl.BlockSpec((1, 1, N_MOD, D_MODEL), lambda i: (0, _mod_row(i), 0, 0)),
            pl.BlockSpec((TM, ATTN_WIDTH), row),
            pl.BlockSpec((TM, FOURIER_WIDTH), row),
            pl.BlockSpec((TM, D_MODEL), row),
            pl.BlockSpec((TM, D_MODEL), row),
            pl.BlockSpec((ATTN_WIDTH, D_MODEL), fix, **const),
            pl.BlockSpec((FOURIER_WIDTH, D_MODEL), fix, **const),
            pl.BlockSpec((D_MODEL, D_MODEL), fix, **const),
        ],
        out_specs=pl.BlockSpec((TM, D_MODEL), row),
        out_shape=jax.ShapeDtypeStruct((n_tiles * TM, D_MODEL), F32),
        compiler_params=pltpu.CompilerParams(
            dimension_semantics=("arbitrary",), vmem_limit_bytes=VMEM_LIMIT),
        name="merge",
    )(x, m_l, attn, four, ga, gf, wa, wf, wo)


def _rope_tables():
    s = jnp.arange(SEQ, dtype=jnp.int32)
    lane = jnp.arange(128, dtype=jnp.int32) % HEAD_DIM
    inv = 1.0 / (ROPE_BASE ** ((2 * (lane % (AXIS_DIM // 2))).astype(F32) / AXIS_DIM))
    pos = jnp.where(lane[None, :] < AXIS_DIM, (s // GRID_W)[:, None], (s % GRID_W)[:, None]).astype(F32)
    ang = pos * inv[None, :]
    cos, sin = jnp.cos(ang), jnp.sin(ang)
    first = (lane % AXIS_DIM) < (AXIS_DIM // 2)
    sa = jnp.where(first[None, :], -sin, 0.0)
    sb = jnp.where(first[None, :], 0.0, sin)
    n_ctx = BATCH * CTX_LEN

    def full(t, fill):
        return jnp.concatenate([t] * BATCH + [jnp.full((n_ctx, 128), fill, F32)], axis=0)

    return full(cos, 1.0), full(sa, 0.0), full(sb, 0.0)


def _angles(num, den):
    return (2.0 * math.pi / den) * (num % den).astype(F32)


def _dft_tables():
    c = jnp.arange(FGROUP_DIM, dtype=jnp.int32)
    ang = _angles(c[:, None] * c[None, :], FGROUP_DIM)
    wc = jnp.concatenate([jnp.cos(ang), -jnp.sin(ang)], axis=1).astype(BF16)

    s2 = jnp.arange(DFT_B, dtype=jnp.int32)[:, None, None]
    k1 = jnp.arange(DFT_A, dtype=jnp.int32)[None, :, None]
    s1 = jnp.arange(DFT_A, dtype=jnp.int32)[None, None, :]
    ang = _angles(k1 * (DFT_B * s1 + s2), SEQ)
    cs, sn = jnp.cos(ang), jnp.sin(ang)
    t1 = jnp.concatenate([jnp.concatenate([cs, sn], axis=2),
                          jnp.concatenate([-sn, cs], axis=2)], axis=1).astype(BF16)

    b = jnp.arange(DFT_B, dtype=jnp.int32)
    ang = _angles(b[:, None] * b[None, :], DFT_B)
    w2 = jnp.concatenate([jnp.cos(ang), jnp.sin(ang)], axis=1).astype(BF16)

    n = jnp.arange(CTX_LEN, dtype=jnp.int32)
    ang = _angles(n[:, None] * n[None, :], CTX_LEN)
    wctx = jnp.concatenate([jnp.cos(ang), jnp.sin(ang)], axis=1).astype(BF16)
    return wc, t1, w2, wctx


def kernel(x, c, ctx, c_ctx, w_mod, b_mod, norm_g, ffn_w_in, ffn_w_out, w_in, b_gate,
           lam_params, subln_g, w_attn_out, w_four_out, w_o, final_g):
    xs = x.reshape(R_LAT, D_MODEL)
    x_ctx = ctx.reshape(BATCH * CTX_LEN, D_MODEL)
    conds = jnp.concatenate([c, c_ctx[None, :], jnp.zeros((8 - BATCH - 1, D_MODEL), F32)], axis=0)
    m_all = _modulation(conds, w_mod, b_mod)
    tabs = _rope_tables()
    wc, t1, w2, wctx = _dft_tables()

    for l in range(DEPTH):
        last = l == DEPTH - 1
        lam_init = 0.8 - 0.6 * math.exp(-0.3 * l)
        m_l = m_all[l:l + 1]
        w_ffn_in = [ffn_w_in[l, i].astype(BF16) for i in range(2)]
        w_ffn_out = [ffn_w_out[l, i].astype(BF16) for i in range(2)]

        xs = _ffn(xs, m_l, norm_g[l, 0], w_ffn_in[0], w_ffn_out[0], sub=0, n_tiles=N_ALL_TILES,
                  x_ctx=x_ctx if l == 0 else None)
        w_v_t = w_in[l, :, 2 * QK_W:F_OFF].T.astype(BF16)
        q, k, vt, f, ga, gf = _proj(xs, m_l, norm_g[l, 1], w_in[l].astype(BF16), w_v_t, tabs, wc, b_gate[l])
        attn = _attention(q, k, vt, lam_params[l], subln_g[l], lam_init=lam_init, with_ctx_queries=not last)
        four = _fourier_latent(f, t1, w2, rows=R_LAT if last else R_ALL)
        if not last:
            four = _fourier_ctx(f, wctx, four)
        n_tiles = N_LAT_TILES if last else N_ALL_TILES
        xs = _merge(xs, m_l, attn, four, ga, gf, w_attn_out[l].astype(BF16), w_four_out[l].astype(BF16),
                    w_o[l].astype(BF16), n_tiles=n_tiles)
        xs = _ffn(xs, m_l, norm_g[l, 2], w_ffn_in[1], w_ffn_out[1], sub=2, n_tiles=n_tiles,
                  final_g=final_g if last else None)
    return xs.reshape(BATCH, SEQ, D_MODEL)
```

```python
import functools
import math

import jax
import jax.numpy as jnp
from jax import lax
from jax.experimental import pallas as pl
from jax.experimental.pallas import tpu as pltpu

D_MODEL = 1024
BATCH = 2
SEQ = 8192
DEPTH = 2
CTX_LEN = 256
GRID_W = 64
N_HEADS = 8
HEAD_DIM = 64
V_DIM = 2 * HEAD_DIM
QK_W = N_HEADS * 2 * HEAD_DIM
ATTN_WIDTH = N_HEADS * V_DIM
N_FGROUPS = 4
FGROUP_DIM = 128
FOURIER_WIDTH = N_FGROUPS * FGROUP_DIM
F_OFF = 2 * QK_W + ATTN_WIDTH
G_OFF = F_OFF + FOURIER_WIDTH
IN_WIDTH = G_OFF + 2 * D_MODEL
D_FF = 2816
AXIS_DIM = HEAD_DIM // 2
ROPE_BASE = 10000.0
N_MOD = 9
EPS = 1e-6

R_LAT = BATCH * SEQ
R_ALL = R_LAT + BATCH * CTX_LEN
TM = 512
N_LAT_TILES = R_LAT // TM
N_ALL_TILES = R_ALL // TM
FF_CHUNK = 256
N_FF_CHUNKS = D_FF // FF_CHUNK
TQ = 256
TK = 1024
N_LAT_BLOCKS = SEQ // TK
PROB_SUM_LIMIT = 2.0 ** 24
N_LAT_QT = SEQ // TQ
DFT_A = 64
DFT_B = 128
NS2 = 8
NK1 = 8
VMEM_LIMIT = 56 * 1024 * 1024

F32 = jnp.float32
BF16 = jnp.bfloat16


def _dot(a, b):
    return jnp.dot(a, b, preferred_element_type=F32)


def _sigmoid(x):
    return 1.0 / (1.0 + jnp.exp(-x))


def _mod_row(i):
    return jnp.minimum(i // (SEQ // TM), 2)


def _norm_mod(x, g, shift, scale):
    var = jnp.mean(x * x, axis=-1, keepdims=True)
    y = x * lax.rsqrt(var + EPS) * g
    return y * (1.0 + scale) + shift


def _mod_kernel(c_ref, w_ref, b_ref, o_ref):
    c = c_ref[...]
    s = c * _sigmoid(c)
    o_ref[0] = jnp.dot(s, w_ref[0], preferred_element_type=F32,
                       precision=lax.Precision.HIGHEST) + b_ref[0]


def _modulation(conds, w_mod, b_mod):
    tn = D_MODEL
    out = pl.pallas_call(
        _mod_kernel,
        grid=(DEPTH, N_MOD),
        in_specs=[
            pl.BlockSpec((8, D_MODEL), lambda l, j: (0, 0)),
            pl.BlockSpec((1, D_MODEL, tn), lambda l, j: (l, 0, j)),
            pl.BlockSpec((1, 1, tn), lambda l, j: (l, 0, j)),
        ],
        out_specs=pl.BlockSpec((1, 8, tn), lambda l, j: (l, 0, j)),
        out_shape=jax.ShapeDtypeStruct((DEPTH, 8, N_MOD * D_MODEL), F32),
        compiler_params=pltpu.CompilerParams(
            dimension_semantics=("arbitrary", "arbitrary"), vmem_limit_bytes=VMEM_LIMIT),
        name="modulation",
    )(conds, w_mod, b_mod.reshape(DEPTH, 1, N_MOD * D_MODEL))
    return out.reshape(DEPTH, 8, N_MOD, D_MODEL)


def _ffn_kernel(*refs, sub, final, split_input):
    refs = list(refs)
    x_ref = refs.pop(0)
    xc_ref = refs.pop(0) if split_input else None
    m_ref, g_ref, win_ref, wout_ref = refs[:4]
    fg_ref = refs[4] if final else None
    o_ref, u_ref, h_ref = refs[-3:]
    x = x_ref[...]
    if split_input:
        x = jnp.where(pl.program_id(0) < N_LAT_TILES, x, xc_ref[...])
    m = m_ref[0, 0]
    shift = m[3 * sub:3 * sub + 1]
    scale = m[3 * sub + 1:3 * sub + 2]
    gate = m[3 * sub + 2:3 * sub + 3]
    u_ref[...] = _norm_mod(x, g_ref[...], shift, scale).astype(BF16)
    for c in range(N_FF_CHUNKS):
        lo = c * FF_CHUNK
        a = _dot(u_ref[...], win_ref[:, lo:lo + FF_CHUNK])
        b = _dot(u_ref[...], win_ref[:, D_FF + lo:D_FF + lo + FF_CHUNK])
        h_ref[:, lo:lo + FF_CHUNK] = (a * _sigmoid(a) * b).astype(BF16)
    y = x + 0.5 * gate * _dot(h_ref[...], wout_ref[...])
    if final:
        var = jnp.mean(y * y, axis=-1, keepdims=True)
        y = y * lax.rsqrt(var + EPS) * fg_ref[...]
    o_ref[...] = y


def _ffn(x, m_l, g, w_in_c, w_out_c, *, sub, n_tiles, final_g=None, x_ctx=None):
    final = final_g is not None
    split_input = x_ctx is not None
    const = dict(pipeline_mode=pl.Buffered(1))
    if split_input:
        in_specs = [pl.BlockSpec((TM, D_MODEL), lambda i: (jnp.minimum(i, N_LAT_TILES - 1), 0)),
                    pl.BlockSpec((TM, D_MODEL), lambda i: (0, 0))]
        args = [x, x_ctx]
    else:
        in_specs = [pl.BlockSpec((TM, D_MODEL), lambda i: (i, 0))]
        args = [x]
    in_specs += [
        pl.BlockSpec((1, 1, N_MOD, D_MODEL), lambda i: (0, _mod_row(i), 0, 0)),
        pl.BlockSpec((1, D_MODEL), lambda i: (0, 0)),
        pl.BlockSpec((D_MODEL, 2 * D_FF), lambda i: (0, 0), **const),
        pl.BlockSpec((D_FF, D_MODEL), lambda i: (0, 0), **const),
    ]
    args += [m_l, g.reshape(1, D_MODEL), w_in_c, w_out_c]
    if final:
        in_specs.append(pl.BlockSpec((1, D_MODEL), lambda i: (0, 0)))
        args.append(final_g.reshape(1, D_MODEL))
    return pl.pallas_call(
        functools.partial(_ffn_kernel, sub=sub, final=final, split_input=split_input),
        grid=(n_tiles,),
        in_specs=in_specs,
        out_specs=pl.BlockSpec((TM, D_MODEL), lambda i: (i, 0)),
        out_shape=jax.ShapeDtypeStruct((n_tiles * TM, D_MODEL), F32),
        scratch_shapes=[pltpu.VMEM((TM, D_MODEL), BF16), pltpu.VMEM((TM, D_FF), BF16)],
        compiler_params=pltpu.CompilerParams(
            dimension_semantics=("arbitrary",), vmem_limit_bytes=VMEM_LIMIT),
        name="ffn",
    )(*args)


def _proj_kernel(x_ref, m_ref, g_ref, w_ref, wvt_ref, cos_ref, sa_ref, sb_ref, wc_ref, bg_ref,
                 q_ref, k_ref, vt_ref, f_ref, ga_ref, gf_ref, u_ref):
    x = x_ref[...]
    m = m_ref[0, 0]
    u_ref[...] = _norm_mod(x, g_ref[...], m[3:4], m[4:5]).astype(BF16)
    cw = 256
    cos = jnp.concatenate([cos_ref[...]] * (cw // 128), axis=1)
    sa = jnp.concatenate([sa_ref[...]] * (cw // 128), axis=1)
    sb = jnp.concatenate([sb_ref[...]] * (cw // 128), axis=1)

    def col(off):
        return _dot(u_ref[...], w_ref[:, off:off + cw])

    for out_ref, base, qscale in ((q_ref, 0, HEAD_DIM ** -0.5 * math.log2(math.e)), (k_ref, QK_W, 1.0)):
        for j in range(QK_W // cw):
            p = col(base + j * cw)
            r = p * cos + pltpu.roll(p, cw - 16, 1) * sa + pltpu.roll(p, 16, 1) * sb
            if qscale != 1.0:
                r = r * qscale
            out_ref[:, j * cw:(j + 1) * cw] = r.astype(BF16)
    for j in range(ATTN_WIDTH // cw):
        vt = lax.dot_general(wvt_ref[j * cw:(j + 1) * cw, :], u_ref[...], (((1,), (1,)), ((), ())),
                             preferred_element_type=F32)
        vt_ref[j * cw:(j + 1) * cw, :] = vt.astype(BF16)
    for j in range(FOURIER_WIDTH // cw):
        p = col(F_OFF + j * cw).astype(BF16)
        for t in range(cw // FGROUP_DIM):
            grp = j * (cw // FGROUP_DIM) + t
            r = _dot(p[:, t * FGROUP_DIM:(t + 1) * FGROUP_DIM], wc_ref[...])
            f_ref[:, grp * FGROUP_DIM:(grp + 1) * FGROUP_DIM] = r[:, :FGROUP_DIM].astype(BF16)
            f_ref[:, FOURIER_WIDTH + grp * FGROUP_DIM:FOURIER_WIDTH + (grp + 1) * FGROUP_DIM] = (
                r[:, FGROUP_DIM:].astype(BF16))
    for out_ref, base, brow in ((ga_ref, G_OFF, 0), (gf_ref, G_OFF + D_MODEL, 1)):
        for j in range(D_MODEL // cw):
            p = col(base + j * cw) + bg_ref[brow:brow + 1, j * cw:(j + 1) * cw]
            out_ref[:, j * cw:(j + 1) * cw] = _sigmoid(p).astype(BF16)


def _proj(x, m_l, g, w_in_b, w_v_t, tabs, wc, b_gate):
    cos, sa, sb = tabs
    const = dict(pipeline_mode=pl.Buffered(1))
    row = lambda i: (i, 0)
    fix = lambda i: (0, 0)
    wide = jax.ShapeDtypeStruct((R_ALL, D_MODEL), BF16)
    row_spec = pl.BlockSpec((TM, D_MODEL), row)
    return pl.pallas_call(
        _proj_kernel,
        grid=(N_ALL_TILES,),
        in_specs=[
            pl.BlockSpec((TM, D_MODEL), row),
            pl.BlockSpec((1, 1, N_MOD, D_MODEL), lambda i: (0, _mod_row(i), 0, 0)),
            pl.BlockSpec((1, D_MODEL), fix),
            pl.BlockSpec((D_MODEL, IN_WIDTH), fix, **const),
            pl.BlockSpec((ATTN_WIDTH, D_MODEL), fix, **const),
            pl.BlockSpec((TM, 128), row),
            pl.BlockSpec((TM, 128), row),
            pl.BlockSpec((TM, 128), row),
            pl.BlockSpec((FGROUP_DIM, 2 * FGROUP_DIM), fix),
            pl.BlockSpec((2, D_MODEL), fix),
        ],
        out_specs=[row_spec, row_spec, pl.BlockSpec((ATTN_WIDTH, TM), lambda i: (0, i)),
                   row_spec, row_spec, row_spec],
        out_shape=[wide, wide, jax.ShapeDtypeStruct((ATTN_WIDTH, R_ALL), BF16), wide, wide, wide],
        scratch_shapes=[pltpu.VMEM((TM, D_MODEL), BF16)],
        compiler_params=pltpu.CompilerParams(
            dimension_semantics=("arbitrary",), vmem_limit_bytes=VMEM_LIMIT),
        name="proj",
    )(x, m_l, g.reshape(1, D_MODEL), w_in_b, w_v_t, cos, sa, sb, wc, b_gate)


def _masked_queries(q):
    lane = lax.broadcasted_iota(jnp.int32, q.shape, 1)
    zero = jnp.zeros_like(q)
    return jnp.where(lane < HEAD_DIM, q, zero), jnp.where(lane >= HEAD_DIM, q, zero)


def _lambda(lp, lam_init):
    return (jnp.exp(jnp.sum(lp[0:1] * lp[1:2], axis=-1, keepdims=True))
            - jnp.exp(jnp.sum(lp[2:3] * lp[3:4], axis=-1, keepdims=True)) + lam_init)


def _attn_epilogue(acc, l, lam, sg, lam_init):
    o = acc / l
    d = o[:, :TQ] - lam * o[:, TQ:]
    var = jnp.mean(d * d, axis=0, keepdims=True)
    y = d * lax.rsqrt(var + EPS) * sg * (1.0 - lam_init)
    return y.T.astype(BF16)


def _attn_kernel(lp_ref, sg_ref, q_ref, kc_ref, vc_ref, kl_ref, vl_ref, o_ref,
                 qs_ref, pacc_ref, pl_ref, qx_ref, m_ref, l_ref, acc_ref, *, lam_init):
    lam = _lambda(lp_ref[...], lam_init)

    def tile_rows(t):
        if isinstance(t, int):
            return pl.ds(t * TQ, TQ)
        return pl.ds(pl.multiple_of(t * TQ, TQ), TQ)

    def stage_queries(dst, t):
        q0, q1 = _masked_queries(q_ref[tile_rows(t), :])
        dst[0:TQ, :] = q0
        dst[TQ:, :] = q1

    def scores(keys, queries_ref):
        return lax.dot_general(keys, queries_ref[...], (((1,), (1,)), ((), ())), preferred_element_type=F32)

    def fold8(fn, a):
        return fn(a.reshape(a.shape[0] // 8, 8, 2 * TQ), axis=0)

    def latent_block(blk):
        return kl_ref[blk * TK:(blk + 1) * TK, :], vl_ref[:, blk * TK:(blk + 1) * TK]

    def store_tile(t, acc, l8):
        l = jnp.sum(l8, axis=0, keepdims=True)
        o_ref[tile_rows(t), :] = _attn_epilogue(acc, l, lam, sg_ref[...], lam_init)

    def exact_tile(t):
        stage_queries(qx_ref, t)
        m_ref[...] = jnp.full_like(m_ref, -jnp.inf)
        l_ref[...] = jnp.zeros_like(l_ref)
        acc_ref[...] = jnp.zeros_like(acc_ref)

        def update(keys, vt):
            s = scores(keys, qx_ref)
            m_old = m_ref[...]
            m_new = jnp.maximum(m_old, jnp.max(s, axis=0, keepdims=True))
            alpha = jnp.exp2(m_old - m_new)
            p = jnp.exp2(s - m_new)
            m_ref[...] = m_new
            l_ref[...] = alpha * l_ref[...] + fold8(jnp.sum, p)
            acc_ref[...] = alpha * acc_ref[...] + _dot(vt, p.astype(BF16))

        def latent(blk, carry):
            off = pl.multiple_of(blk * TK, TK)
            update(kl_ref[pl.ds(off, TK), :], vl_ref[:, pl.ds(off, TK)])
            return carry

        update(kc_ref[...], vc_ref[...])
        lax.fori_loop(0, N_LAT_BLOCKS, latent, 0)
        store_tile(t, acc_ref[...], l_ref[...])

    def tile(t, redo_prev):
        stage_queries(qs_ref, t)
        s = scores(kc_ref[...], qs_ref)
        m0 = jnp.max(s, axis=0, keepdims=True)
        p = jnp.exp2(s - m0)
        l8 = fold8(jnp.sum, p)
        acc = _dot(vc_ref[...], p.astype(BF16))
        for blk in range(N_LAT_BLOCKS):
            keys, vt = latent_block(blk)
            p = jnp.exp2(scores(keys, qs_ref) - m0)
            l8 = l8 + fold8(jnp.sum, p)
            acc = acc + _dot(vt, p.astype(BF16))
        prev = jnp.maximum(t - 1, 0)
        store_tile(prev, pacc_ref[...], pl_ref[...])
        pacc_ref[...] = acc
        pl_ref[...] = l8
        redo = (jnp.max(l8) > PROB_SUM_LIMIT).astype(jnp.int32)

        @pl.when(redo_prev == 1)
        def _():
            exact_tile(prev)

        return redo

    pacc_ref[...] = jnp.ones_like(pacc_ref)
    pl_ref[...] = jnp.ones_like(pl_ref)
    redo_last = lax.fori_loop(0, N_LAT_QT, tile, jnp.int32(0))
    store_tile(N_LAT_QT - 1, pacc_ref[...], pl_ref[...])

    @pl.when(redo_last == 1)
    def _():
        exact_tile(N_LAT_QT - 1)


def _attn_ctx_kernel(lp_ref, sg_ref, q_ref, k_ref, vt_ref, buf_ref, o_ref, *, lam_init):
    del buf_ref
    q0, q1 = _masked_queries(q_ref[...])
    qs = jnp.concatenate([q0, q1], axis=0)
    s = lax.dot_general(k_ref[...], qs, (((1,), (1,)), ((), ())), preferred_element_type=F32)
    p = jnp.exp2(s - jnp.max(s, axis=0, keepdims=True))
    l = jnp.sum(p, axis=0, keepdims=True)
    acc = _dot(vt_ref[...], p.astype(BF16))
    o_ref[...] = _attn_epilogue(acc, l, _lambda(lp_ref[...], lam_init), sg_ref[...], lam_init)


def _attention(q, k, vt, lam_params_l, subln_g_l, *, lam_init, with_ctx_queries):
    ctx_blk0 = R_LAT // CTX_LEN
    rows = R_ALL if with_ctx_queries else R_LAT
    sg = subln_g_l.reshape(V_DIM, 1)
    attn = pl.pallas_call(
        functools.partial(_attn_kernel, lam_init=lam_init),
        grid=(BATCH, N_HEADS),
        in_specs=[
            pl.BlockSpec((4, HEAD_DIM), lambda b, h: (0, 0)),
            pl.BlockSpec((V_DIM, 1), lambda b, h: (0, 0)),
            pl.BlockSpec((SEQ, V_DIM), lambda b, h: (b, h)),
            pl.BlockSpec((CTX_LEN, V_DIM), lambda b, h: (ctx_blk0 + b, h)),
            pl.BlockSpec((V_DIM, CTX_LEN), lambda b, h: (h, ctx_blk0 + b)),
            pl.BlockSpec((SEQ, V_DIM), lambda b, h: (b, h)),
            pl.BlockSpec((V_DIM, SEQ), lambda b, h: (h, b)),
        ],
        out_specs=pl.BlockSpec((SEQ, V_DIM), lambda b, h: (b, h)),
        out_shape=jax.ShapeDtypeStruct((rows, ATTN_WIDTH), BF16),
        scratch_shapes=[
            pltpu.VMEM((2 * TQ, V_DIM), BF16),
            pltpu.VMEM((V_DIM, 2 * TQ), F32),
            pltpu.VMEM((8, 2 * TQ), F32),
            pltpu.VMEM((2 * TQ, V_DIM), BF16),
            pltpu.VMEM((1, 2 * TQ), F32),
            pltpu.VMEM((8, 2 * TQ), F32),
            pltpu.VMEM((V_DIM, 2 * TQ), F32),
        ],
        compiler_params=pltpu.CompilerParams(
            dimension_semantics=("arbitrary", "arbitrary"), vmem_limit_bytes=VMEM_LIMIT),
        name="diff_attn",
    )(lam_params_l, sg, q, k, vt, k, vt)
    if not with_ctx_queries:
        return attn
    return pl.pallas_call(
        functools.partial(_attn_ctx_kernel, lam_init=lam_init),
        grid=(BATCH, N_HEADS),
        in_specs=[
            pl.BlockSpec((4, HEAD_DIM), lambda b, h: (0, 0)),
            pl.BlockSpec((V_DIM, 1), lambda b, h: (0, 0)),
            pl.BlockSpec((CTX_LEN, V_DIM), lambda b, h: (ctx_blk0 + b, h)),
            pl.BlockSpec((CTX_LEN, V_DIM), lambda b, h: (ctx_blk0 + b, h)),
            pl.BlockSpec((V_DIM, CTX_LEN), lambda b, h: (h, ctx_blk0 + b)),
            pl.BlockSpec(memory_space=pl.ANY),
        ],
        out_specs=pl.BlockSpec((CTX_LEN, V_DIM), lambda b, h: (ctx_blk0 + b, h)),
        out_shape=jax.ShapeDtypeStruct((R_ALL, ATTN_WIDTH), BF16),
        input_output_aliases={5: 0},
        compiler_params=pltpu.CompilerParams(
            dimension_semantics=("arbitrary", "arbitrary"), vmem_limit_bytes=VMEM_LIMIT),
        name="diff_attn_ctx",
    )(lam_params_l, sg, q, k, vt, attn)


def _dft1_kernel(x_ref, t_ref, o_ref):
    w = 2 * FOURIER_WIDTH
    for j in range(NS2):
        xs = jnp.concatenate([x_ref[:, j * w:j * w + FOURIER_WIDTH],
                              x_ref[:, j * w + FOURIER_WIDTH:(j + 1) * w]], axis=0)
        r = _dot(t_ref[j], xs)
        o_ref[:, j * w:j * w + FOURIER_WIDTH] = r[:DFT_A].astype(BF16)
        o_ref[:, j * w + FOURIER_WIDTH:(j + 1) * w] = r[DFT_A:].astype(BF16)


def _dft2_kernel(x_ref, w_ref, o_ref, *, n, nk, scale):
    for j in range(nk):
        xs = jnp.concatenate([x_ref[j * n:(j + 1) * n, :FOURIER_WIDTH],
                              x_ref[j * n:(j + 1) * n, FOURIER_WIDTH:]], axis=0)
        r = _dot(w_ref[...], xs) * scale
        o_ref[:, j * FOURIER_WIDTH:(j + 1) * FOURIER_WIDTH] = r.astype(BF16)


def _fourier_latent(f_all, t1, w2, *, rows):
    width = DFT_B * 2 * FOURIER_WIDTH
    x2 = pl.pallas_call(
        _dft1_kernel,
        grid=(BATCH, DFT_B // NS2),
        in_specs=[
            pl.BlockSpec((DFT_A, NS2 * 2 * FOURIER_WIDTH), lambda b, j: (b, j)),
            pl.BlockSpec((NS2, 2 * DFT_A, 2 * DFT_A), lambda b, j: (j, 0, 0)),
        ],
        out_specs=pl.BlockSpec((DFT_A, NS2 * 2 * FOURIER_WIDTH), lambda b, j: (b, j)),
        out_shape=jax.ShapeDtypeStruct((BATCH * DFT_A, width), BF16),
        compiler_params=pltpu.CompilerParams(
            dimension_semantics=("arbitrary", "arbitrary"), vmem_limit_bytes=VMEM_LIMIT),
        name="dft_stage1",
    )(f_all.reshape(R_ALL // DFT_B, width), t1)
    y = pl.pallas_call(
        functools.partial(_dft2_kernel, n=DFT_B, nk=NK1, scale=(SEQ * FGROUP_DIM) ** -0.5),
        grid=(BATCH, DFT_A // NK1),
        in_specs=[
            pl.BlockSpec((NK1 * DFT_B, 2 * FOURIER_WIDTH), lambda b, j: (b * (DFT_A // NK1) + j, 0)),
            pl.BlockSpec((DFT_B, 2 * DFT_B), lambda b, j: (0, 0)),
        ],
        out_specs=pl.BlockSpec((DFT_B, NK1 * FOURIER_WIDTH), lambda b, j: (b, j)),
        out_shape=jax.ShapeDtypeStruct((rows // DFT_A, DFT_A * FOURIER_WIDTH), BF16),
        compiler_params=pltpu.CompilerParams(
            dimension_semantics=("arbitrary", "arbitrary"), vmem_limit_bytes=VMEM_LIMIT),
        name="dft_stage2",
    )(x2.reshape(R_LAT, 2 * FOURIER_WIDTH), w2)
    return y.reshape(rows, FOURIER_WIDTH)


def _dft_ctx_kernel(x_ref, w_ref, buf_ref, o_ref, **kw):
    del buf_ref
    _dft2_kernel(x_ref, w_ref, o_ref, **kw)


def _fourier_ctx(f_all, wctx, four):
    ctx_blk0 = R_LAT // CTX_LEN
    return pl.pallas_call(
        functools.partial(_dft_ctx_kernel, n=CTX_LEN, nk=1, scale=(CTX_LEN * FGROUP_DIM) ** -0.5),
        grid=(BATCH,),
        in_specs=[
            pl.BlockSpec((CTX_LEN, 2 * FOURIER_WIDTH), lambda b: (ctx_blk0 + b, 0)),
            pl.BlockSpec((CTX_LEN, 2 * CTX_LEN), lambda b: (0, 0)),
            pl.BlockSpec(memory_space=pl.ANY),
        ],
        out_specs=pl.BlockSpec((CTX_LEN, FOURIER_WIDTH), lambda b: (ctx_blk0 + b, 0)),
        out_shape=jax.ShapeDtypeStruct((R_ALL, FOURIER_WIDTH), BF16),
        input_output_aliases={2: 0},
        compiler_params=pltpu.CompilerParams(
            dimension_semantics=("arbitrary",), vmem_limit_bytes=VMEM_LIMIT),
        name="dft_ctx",
    )(f_all, wctx, four)


def _merge_kernel(x_ref, m_ref, a_ref, f_ref, ga_ref, gf_ref, wa_ref, wf_ref, wo_ref, o_ref):
    a = _dot(a_ref[...], wa_ref[...])
    fr = _dot(f_ref[...], wf_ref[...])
    merged = ga_ref[...].astype(F32) * a + gf_ref[...].astype(F32) * fr
    mix = _dot(merged.astype(BF16), wo_ref[...])
    o_ref[...] = x_ref[...] + m_ref[0, 0][5:6] * mix


def _merge(x, m_l, attn, four, ga, gf, wa, wf, wo, *, n_tiles):
    const = dict(pipeline_mode=pl.Buffered(1))
    row = lambda i: (i, 0)
    fix = lambda i: (0, 0)
    return pl.pallas_call(
        _merge_kernel,
        grid=(n_tiles,),
        in_specs=[
            pl.BlockSpec((TM, D_MODEL), row),
            pl.BlockSpec((1, 1, N_MOD, D_MODEL), lambda i: (0, _mod_row(i), 0, 0)),
            pl.BlockSpec((TM, ATTN_WIDTH), row),
            pl.BlockSpec((TM, FOURIER_WIDTH), row),
            pl.BlockSpec((TM, D_MODEL), row),
            pl.BlockSpec((TM, D_MODEL), row),
            pl.BlockSpec((ATTN_WIDTH, D_MODEL), fix, **const),
            pl.BlockSpec((FOURIER_WIDTH, D_MODEL), fix, **const),
            pl.BlockSpec((D_MODEL, D_MODEL), fix, **const),
        ],
        out_specs=pl.BlockSpec((TM, D_MODEL), row),
        out_shape=jax.ShapeDtypeStruct((n_tiles * TM, D_MODEL), F32),
        compiler_params=pltpu.CompilerParams(
            dimension_semantics=("arbitrary",), vmem_limit_bytes=VMEM_LIMIT),
        name="merge",
    )(x, m_l, attn, four, ga, gf, wa, wf, wo)


def _rope_tables():
    s = jnp.arange(SEQ, dtype=jnp.int32)
    lane = jnp.arange(128, dtype=jnp.int32) % HEAD_DIM
    inv = 1.0 / (ROPE_BASE ** ((2 * (lane % (AXIS_DIM // 2))).astype(F32) / AXIS_DIM))
    pos = jnp.where(lane[None, :] < AXIS_DIM, (s // GRID_W)[:, None], (s % GRID_W)[:, None]).astype(F32)
    ang = pos * inv[None, :]
    cos, sin = jnp.cos(ang), jnp.sin(ang)
    first = (lane % AXIS_DIM) < (AXIS_DIM // 2)
    sa = jnp.where(first[None, :], -sin, 0.0)
    sb = jnp.where(first[None, :], 0.0, sin)
    n_ctx = BATCH * CTX_LEN

    def full(t, fill):
        return jnp.concatenate([t] * BATCH + [jnp.full((n_ctx, 128), fill, F32)], axis=0)

    return full(cos, 1.0), full(sa, 0.0), full(sb, 0.0)


def _angles(num, den):
    return (2.0 * math.pi / den) * (num % den).astype(F32)


def _dft_tables():
    c = jnp.arange(FGROUP_DIM, dtype=jnp.int32)
    ang = _angles(c[:, None] * c[None, :], FGROUP_DIM)
    wc = jnp.concatenate([jnp.cos(ang), -jnp.sin(ang)], axis=1).astype(BF16)

    s2 = jnp.arange(DFT_B, dtype=jnp.int32)[:, None, None]
    k1 = jnp.arange(DFT_A, dtype=jnp.int32)[None, :, None]
    s1 = jnp.arange(DFT_A, dtype=jnp.int32)[None, None, :]
    ang = _angles(k1 * (DFT_B * s1 + s2), SEQ)
    cs, sn = jnp.cos(ang), jnp.sin(ang)
    t1 = jnp.concatenate([jnp.concatenate([cs, sn], axis=2),
                          jnp.concatenate([-sn, cs], axis=2)], axis=1).astype(BF16)

    b = jnp.arange(DFT_B, dtype=jnp.int32)
    ang = _angles(b[:, None] * b[None, :], DFT_B)
    w2 = jnp.concatenate([jnp.cos(ang), jnp.sin(ang)], axis=1).astype(BF16)

    n = jnp.arange(CTX_LEN, dtype=jnp.int32)
    ang = _angles(n[:, None] * n[None, :], CTX_LEN)
    wctx = jnp.concatenate([jnp.cos(ang), jnp.sin(ang)], axis=1).astype(BF16)
    return wc, t1, w2, wctx


def kernel(x, c, ctx, c_ctx, w_mod, b_mod, norm_g, ffn_w_in, ffn_w_out, w_in, b_gate,
           lam_params, subln_g, w_attn_out, w_four_out, w_o, final_g):
    xs = x.reshape(R_LAT, D_MODEL)
    x_ctx = ctx.reshape(BATCH * CTX_LEN, D_MODEL)
    conds = jnp.concatenate([c, c_ctx[None, :], jnp.zeros((8 - BATCH - 1, D_MODEL), F32)], axis=0)
    m_all = _modulation(conds, w_mod, b_mod)
    tabs = _rope_tables()
    wc, t1, w2, wctx = _dft_tables()

    for l in range(DEPTH):
        last = l == DEPTH - 1
        lam_init = 0.8 - 0.6 * math.exp(-0.3 * l)
        m_l = m_all[l:l + 1]
        w_ffn_in = [ffn_w_in[l, i].astype(BF16) for i in range(2)]
        w_ffn_out = [ffn_w_out[l, i].astype(BF16) for i in range(2)]

        xs = _ffn(xs, m_l, norm_g[l, 0], w_ffn_in[0], w_ffn_out[0], sub=0, n_tiles=N_ALL_TILES,
                  x_ctx=x_ctx if l == 0 else None)
        w_v_t = w_in[l, :, 2 * QK_W:F_OFF].T.astype(BF16)
        q, k, vt, f, ga, gf = _proj(xs, m_l, norm_g[l, 1], w_in[l].astype(BF16), w_v_t, tabs, wc, b_gate[l])
        attn = _attention(q, k, vt, lam_params[l], subln_g[l], lam_init=lam_init, with_ctx_queries=not last)
        four = _fourier_latent(f, t1, w2, rows=R_LAT if last else R_ALL)
        if not last:
            four = _fourier_ctx(f, wctx, four)
        n_tiles = N_LAT_TILES if last else N_ALL_TILES
        xs = _merge(xs, m_l, attn, four, ga, gf, w_attn_out[l].astype(BF16), w_four_out[l].astype(BF16),
                    w_o[l].astype(BF16), n_tiles=n_tiles)
        xs = _ffn(xs, m_l, norm_g[l, 2], w_ffn_in[1], w_ffn_out[1], sub=2, n_tiles=n_tiles,
                  final_g=final_g if last else None)
    return xs.reshape(BATCH, SEQ, D_MODEL)
```

```python
import functools
import math

import jax
import jax.numpy as jnp
from jax import lax
from jax.experimental import pallas as pl
from jax.experimental.pallas import tpu as pltpu

D_MODEL = 1024
BATCH = 2
SEQ = 8192
DEPTH = 2
CTX_LEN = 256
GRID_W = 64
N_HEADS = 8
HEAD_DIM = 64
V_DIM = 2 * HEAD_DIM
QK_W = N_HEADS * 2 * HEAD_DIM
ATTN_WIDTH = N_HEADS * V_DIM
N_FGROUPS = 4
FGROUP_DIM = 128
FOURIER_WIDTH = N_FGROUPS * FGROUP_DIM
F_OFF = 2 * QK_W + ATTN_WIDTH
G_OFF = F_OFF + FOURIER_WIDTH
IN_WIDTH = G_OFF + 2 * D_MODEL
D_FF = 2816
AXIS_DIM = HEAD_DIM // 2
ROPE_BASE = 10000.0
N_MOD = 9
EPS = 1e-6

R_LAT = BATCH * SEQ
R_ALL = R_LAT + BATCH * CTX_LEN
TM = 512
N_LAT_TILES = R_LAT // TM
N_ALL_TILES = R_ALL // TM
FF_CHUNK = 256
N_FF_CHUNKS = D_FF // FF_CHUNK
TQ = 256
TK = 1024
N_LAT_BLOCKS = SEQ // TK
PROB_SUM_LIMIT = 2.0 ** 24
N_LAT_QT = SEQ // TQ
DFT_A = 64
DFT_B = 128
NS2 = 16
NK1 = 8
VMEM_LIMIT = 56 * 1024 * 1024

F32 = jnp.float32
BF16 = jnp.bfloat16


def _dot(a, b):
    return jnp.dot(a, b, preferred_element_type=F32)


def _sigmoid(x):
    return 1.0 / (1.0 + jnp.exp(-x))


def _mod_row(i):
    return jnp.minimum(i // (SEQ // TM), 2)


def _norm_mod(x, g, shift, scale):
    var = jnp.mean(x * x, axis=-1, keepdims=True)
    y = x * lax.rsqrt(var + EPS) * g
    return y * (1.0 + scale) + shift


def _mod_kernel(c_ref, w_ref, b_ref, o_ref):
    c = c_ref[...]
    s = c * _sigmoid(c)
    o_ref[0] = jnp.dot(s, w_ref[0], preferred_element_type=F32,
                       precision=lax.Precision.HIGHEST) + b_ref[0]


def _modulation(conds, w_mod, b_mod):
    tn = D_MODEL
    out = pl.pallas_call(
        _mod_kernel,
        grid=(DEPTH, N_MOD),
        in_specs=[
            pl.BlockSpec((8, D_MODEL), lambda l, j: (0, 0)),
            pl.BlockSpec((1, D_MODEL, tn), lambda l, j: (l, 0, j)),
            pl.BlockSpec((1, 1, tn), lambda l, j: (l, 0, j)),
        ],
        out_specs=pl.BlockSpec((1, 8, tn), lambda l, j: (l, 0, j)),
        out_shape=jax.ShapeDtypeStruct((DEPTH, 8, N_MOD * D_MODEL), F32),
        compiler_params=pltpu.CompilerParams(
            dimension_semantics=("arbitrary", "arbitrary"), vmem_limit_bytes=VMEM_LIMIT),
        name="modulation",
    )(conds, w_mod, b_mod.reshape(DEPTH, 1, N_MOD * D_MODEL))
    return out.reshape(DEPTH, 8, N_MOD, D_MODEL)


def _ffn_kernel(*refs, sub, final, split_input):
    refs = list(refs)
    x_ref = refs.pop(0)
    xc_ref = refs.pop(0) if split_input else None
    m_ref, g_ref, win_ref, wout_ref = refs[:4]
    fg_ref = refs[4] if final else None
    o_ref, u_ref, h_ref = refs[-3:]
    x = x_ref[...]
    if split_input:
        x = jnp.where(pl.program_id(0) < N_LAT_TILES, x, xc_ref[...])
    m = m_ref[0, 0]
    shift = m[3 * sub:3 * sub + 1]
    scale = m[3 * sub + 1:3 * sub + 2]
    gate = m[3 * sub + 2:3 * sub + 3]
    u_ref[...] = _norm_mod(x, g_ref[...], shift, scale).astype(BF16)
    for c in range(N_FF_CHUNKS):
        lo = c * FF_CHUNK
        a = _dot(u_ref[...], win_ref[0, 0, :, lo:lo + FF_CHUNK])
        b = _dot(u_ref[...], win_ref[0, 0, :, D_FF + lo:D_FF + lo + FF_CHUNK])
        h_ref[:, lo:lo + FF_CHUNK] = (a * _sigmoid(a) * b).astype(BF16)
    y = x + 0.5 * gate * _dot(h_ref[...], wout_ref[0, 0])
    if final:
        var = jnp.mean(y * y, axis=-1, keepdims=True)
        y = y * lax.rsqrt(var + EPS) * fg_ref[...]
    o_ref[...] = y


def _ffn(x, m_l, g, w_in_c, w_out_c, *, wsel, sub, n_tiles, final_g=None, x_ctx=None):
    final = final_g is not None
    split_input = x_ctx is not None
    const = dict(pipeline_mode=pl.Buffered(1))
    if split_input:
        in_specs = [pl.BlockSpec((TM, D_MODEL), lambda i: (jnp.minimum(i, N_LAT_TILES - 1), 0)),
                    pl.BlockSpec((TM, D_MODEL), lambda i: (0, 0))]
        args = [x, x_ctx]
    else:
        in_specs = [pl.BlockSpec((TM, D_MODEL), lambda i: (i, 0))]
        args = [x]
    in_specs += [
        pl.BlockSpec((1, 1, N_MOD, D_MODEL), lambda i: (0, _mod_row(i), 0, 0)),
        pl.BlockSpec((1, D_MODEL), lambda i: (0, 0)),
        pl.BlockSpec((1, 1, D_MODEL, 2 * D_FF), lambda i: wsel + (0, 0), **const),
        pl.BlockSpec((1, 1, D_FF, D_MODEL), lambda i: wsel + (0, 0), **const),
    ]
    args += [m_l, g.reshape(1, D_MODEL), w_in_c, w_out_c]
    if final:
        in_specs.append(pl.BlockSpec((1, D_MODEL), lambda i: (0, 0)))
        args.append(final_g.reshape(1, D_MODEL))
    return pl.pallas_call(
        functools.partial(_ffn_kernel, sub=sub, final=final, split_input=split_input),
        grid=(n_tiles,),
        in_specs=in_specs,
        out_specs=pl.BlockSpec((TM, D_MODEL), lambda i: (i, 0)),
        out_shape=jax.ShapeDtypeStruct((n_tiles * TM, D_MODEL), F32),
        scratch_shapes=[pltpu.VMEM((TM, D_MODEL), BF16), pltpu.VMEM((TM, D_FF), BF16)],
        compiler_params=pltpu.CompilerParams(
            dimension_semantics=("arbitrary",), vmem_limit_bytes=VMEM_LIMIT),
        name="ffn",
    )(*args)


def _proj_kernel(x_ref, m_ref, g_ref, w_ref, wvt_ref, cos_ref, sa_ref, sb_ref, wc_ref, bg_ref,
                 q_ref, k_ref, vt_ref, f_ref, ga_ref, gf_ref, u_ref):
    x = x_ref[...]
    m = m_ref[0, 0]
    u_ref[...] = _norm_mod(x, g_ref[...], m[3:4], m[4:5]).astype(BF16)
    cw = 256
    cos = jnp.concatenate([cos_ref[...]] * (cw // 128), axis=1)
    sa = jnp.concatenate([sa_ref[...]] * (cw // 128), axis=1)
    sb = jnp.concatenate([sb_ref[...]] * (cw // 128), axis=1)

    def col(off):
        return _dot(u_ref[...], w_ref[0, :, off:off + cw])

    for out_ref, base, qscale in ((q_ref, 0, HEAD_DIM ** -0.5 * math.log2(math.e)), (k_ref, QK_W, 1.0)):
        for j in range(QK_W // cw):
            p = col(base + j * cw)
            r = p * cos + pltpu.roll(p, cw - 16, 1) * sa + pltpu.roll(p, 16, 1) * sb
            if qscale != 1.0:
                r = r * qscale
            out_ref[:, j * cw:(j + 1) * cw] = r.astype(BF16)
    for j in range(ATTN_WIDTH // cw):
        vt = lax.dot_general(wvt_ref[0, j * cw:(j + 1) * cw, :], u_ref[...], (((1,), (1,)), ((), ())),
                             preferred_element_type=F32)
        vt_ref[j * cw:(j + 1) * cw, :] = vt.astype(BF16)
    for j in range(FOURIER_WIDTH // cw):
        p = col(F_OFF + j * cw).astype(BF16)
        for t in range(cw // FGROUP_DIM):
            grp = j * (cw // FGROUP_DIM) + t
            r = _dot(p[:, t * FGROUP_DIM:(t + 1) * FGROUP_DIM], wc_ref[...])
            f_ref[:, grp * FGROUP_DIM:(grp + 1) * FGROUP_DIM] = r[:, :FGROUP_DIM].astype(BF16)
            f_ref[:, FOURIER_WIDTH + grp * FGROUP_DIM:FOURIER_WIDTH + (grp + 1) * FGROUP_DIM] = (
                r[:, FGROUP_DIM:].astype(BF16))
    for out_ref, base, brow in ((ga_ref, G_OFF, 0), (gf_ref, G_OFF + D_MODEL, 1)):
        for j in range(D_MODEL // cw):
            p = col(base + j * cw) + bg_ref[brow:brow + 1, j * cw:(j + 1) * cw]
            out_ref[:, j * cw:(j + 1) * cw] = _sigmoid(p).astype(BF16)


def _proj(x, m_l, g, w_in_b, w_v_t, tabs, wc, b_gate, *, layer):
    cos, sa, sb = tabs
    const = dict(pipeline_mode=pl.Buffered(1))
    row = lambda i: (i, 0)
    fix = lambda i: (0, 0)
    wide = jax.ShapeDtypeStruct((R_ALL, D_MODEL), BF16)
    row_spec = pl.BlockSpec((TM, D_MODEL), row)
    return pl.pallas_call(
        _proj_kernel,
        grid=(N_ALL_TILES,),
        in_specs=[
            pl.BlockSpec((TM, D_MODEL), row),
            pl.BlockSpec((1, 1, N_MOD, D_MODEL), lambda i: (0, _mod_row(i), 0, 0)),
            pl.BlockSpec((1, D_MODEL), fix),
            pl.BlockSpec((1, D_MODEL, IN_WIDTH), lambda i: (layer, 0, 0), **const),
            pl.BlockSpec((1, ATTN_WIDTH, D_MODEL), lambda i: (layer, 0, 0), **const),
            pl.BlockSpec((TM, 128), row),
            pl.BlockSpec((TM, 128), row),
            pl.BlockSpec((TM, 128), row),
            pl.BlockSpec((FGROUP_DIM, 2 * FGROUP_DIM), fix),
            pl.BlockSpec((2, D_MODEL), fix),
        ],
        out_specs=[row_spec, row_spec, pl.BlockSpec((ATTN_WIDTH, TM), lambda i: (0, i)),
                   row_spec, row_spec, row_spec],
        out_shape=[wide, wide, jax.ShapeDtypeStruct((ATTN_WIDTH, R_ALL), BF16), wide, wide, wide],
        scratch_shapes=[pltpu.VMEM((TM, D_MODEL), BF16)],
        compiler_params=pltpu.CompilerParams(
            dimension_semantics=("arbitrary",), vmem_limit_bytes=VMEM_LIMIT),
        name="proj",
    )(x, m_l, g.reshape(1, D_MODEL), w_in_b, w_v_t, cos, sa, sb, wc, b_gate)


def _masked_queries(q):
    lane = lax.broadcasted_iota(jnp.int32, q.shape, 1)
    zero = jnp.zeros_like(q)
    return jnp.where(lane < HEAD_DIM, q, zero), jnp.where(lane >= HEAD_DIM, q, zero)


def _lambda(lp, lam_init):
    return (jnp.exp(jnp.sum(lp[0:1] * lp[1:2], axis=-1, keepdims=True))
            - jnp.exp(jnp.sum(lp[2:3] * lp[3:4], axis=-1, keepdims=True)) + lam_init)


def _attn_epilogue(acc, l, lam, sg, lam_init):
    n = acc.shape[1] // 2
    o = acc / l
    d = o[:, :n] - lam * o[:, n:]
    var = jnp.mean(d * d, axis=0, keepdims=True)
    y = d * lax.rsqrt(var + EPS) * sg * (1.0 - lam_init)
    return y.T.astype(BF16)


def _attn_kernel(lp_ref, sg_ref, q_ref, kc_ref, vc_ref, kl_ref, vl_ref, o_ref,
                 qs_ref, pacc_ref, pl_ref, qx_ref, m_ref, l_ref, acc_ref, *, lam_init):
    lam = _lambda(lp_ref[...], lam_init)

    def tile_rows(t):
        if isinstance(t, int):
            return pl.ds(t * TQ, TQ)
        return pl.ds(pl.multiple_of(t * TQ, TQ), TQ)

    def stage_queries(dst, t):
        q0, q1 = _masked_queries(q_ref[tile_rows(t), :])
        dst[0:TQ, :] = q0
        dst[TQ:, :] = q1

    def scores(keys, queries_ref):
        return lax.dot_general(keys, queries_ref[...], (((1,), (1,)), ((), ())), preferred_element_type=F32)

    def fold8(fn, a):
        return fn(a.reshape(a.shape[0] // 8, 8, 2 * TQ), axis=0)

    def latent_block(blk):
        return kl_ref[blk * TK:(blk + 1) * TK, :], vl_ref[:, blk * TK:(blk + 1) * TK]

    def store_tile(t, acc, l8):
        l = jnp.sum(l8, axis=0, keepdims=True)
        o_ref[tile_rows(t), :] = _attn_epilogue(acc, l, lam, sg_ref[...], lam_init)

    def exact_tile(t):
        stage_queries(qx_ref, t)
        m_ref[...] = jnp.full_like(m_ref, -jnp.inf)
        l_ref[...] = jnp.zeros_like(l_ref)
        acc_ref[...] = jnp.zeros_like(acc_ref)

        def update(keys, vt):
            s = scores(keys, qx_ref)
            m_old = m_ref[...]
            m_new = jnp.maximum(m_old, jnp.max(s, axis=0, keepdims=True))
            alpha = jnp.exp2(m_old - m_new)
            p = jnp.exp2(s - m_new)
            m_ref[...] = m_new
            l_ref[...] = alpha * l_ref[...] + fold8(jnp.sum, p)
            acc_ref[...] = alpha * acc_ref[...] + _dot(vt, p.astype(BF16))

        def latent(blk, carry):
            off = pl.multiple_of(blk * TK, TK)
            update(kl_ref[pl.ds(off, TK), :], vl_ref[:, pl.ds(off, TK)])
            return carry

        update(kc_ref[...], vc_ref[...])
        lax.fori_loop(0, N_LAT_BLOCKS, latent, 0)
        store_tile(t, acc_ref[...], l_ref[...])

    def tile(t, redo_prev):
        stage_queries(qs_ref, t)
        s = scores(kc_ref[...], qs_ref)
        m0 = jnp.max(s, axis=0, keepdims=True)
        p = jnp.exp2(s - m0)
        l8 = fold8(jnp.sum, p)
        acc = _dot(vc_ref[...], p.astype(BF16))
        for blk in range(N_LAT_BLOCKS):
            keys, vt = latent_block(blk)
            p = jnp.exp2(scores(keys, qs_ref) - m0)
            l8 = l8 + fold8(jnp.sum, p)
            acc = acc + _dot(vt, p.astype(BF16))
        prev = jnp.maximum(t - 1, 0)
        store_tile(prev, pacc_ref[...], pl_ref[...])
        pacc_ref[...] = acc
        pl_ref[...] = l8
        redo = (jnp.max(l8) > PROB_SUM_LIMIT).astype(jnp.int32)

        @pl.when(redo_prev == 1)
        def _():
            exact_tile(prev)

        return redo

    pacc_ref[...] = jnp.ones_like(pacc_ref)
    pl_ref[...] = jnp.ones_like(pl_ref)
    redo_last = lax.fori_loop(0, N_LAT_QT, tile, jnp.int32(0))
    store_tile(N_LAT_QT - 1, pacc_ref[...], pl_ref[...])

    @pl.when(redo_last == 1)
    def _():
        exact_tile(N_LAT_QT - 1)


def _attn_ctx_kernel(lp_ref, sg_ref, q_ref, k_ref, vt_ref, o_ref, *, lam_init):
    q0, q1 = _masked_queries(q_ref[...])
    qs = jnp.concatenate([q0, q1], axis=0)
    s = lax.dot_general(k_ref[...], qs, (((1,), (1,)), ((), ())), preferred_element_type=F32)
    p = jnp.exp2(s - jnp.max(s, axis=0, keepdims=True))
    l = jnp.sum(p, axis=0, keepdims=True)
    acc = _dot(vt_ref[...], p.astype(BF16))
    o_ref[...] = _attn_epilogue(acc, l, _lambda(lp_ref[...], lam_init), sg_ref[...], lam_init)


def _attention(q, k, vt, lam_params_l, subln_g_l, *, lam_init, with_ctx_queries):
    ctx_blk0 = R_LAT // CTX_LEN
    sg = subln_g_l.reshape(V_DIM, 1)
    attn = pl.pallas_call(
        functools.partial(_attn_kernel, lam_init=lam_init),
        grid=(BATCH, N_HEADS),
        in_specs=[
            pl.BlockSpec((4, HEAD_DIM), lambda b, h: (0, 0)),
            pl.BlockSpec((V_DIM, 1), lambda b, h: (0, 0)),
            pl.BlockSpec((SEQ, V_DIM), lambda b, h: (b, h)),
            pl.BlockSpec((CTX_LEN, V_DIM), lambda b, h: (ctx_blk0 + b, h)),
            pl.BlockSpec((V_DIM, CTX_LEN), lambda b, h: (h, ctx_blk0 + b)),
            pl.BlockSpec((SEQ, V_DIM), lambda b, h: (b, h)),
            pl.BlockSpec((V_DIM, SEQ), lambda b, h: (h, b)),
        ],
        out_specs=pl.BlockSpec((SEQ, V_DIM), lambda b, h: (b, h)),
        out_shape=jax.ShapeDtypeStruct((R_LAT, ATTN_WIDTH), BF16),
        scratch_shapes=[
            pltpu.VMEM((2 * TQ, V_DIM), BF16),
            pltpu.VMEM((V_DIM, 2 * TQ), F32),
            pltpu.VMEM((8, 2 * TQ), F32),
            pltpu.VMEM((2 * TQ, V_DIM), BF16),
            pltpu.VMEM((1, 2 * TQ), F32),
            pltpu.VMEM((8, 2 * TQ), F32),
            pltpu.VMEM((V_DIM, 2 * TQ), F32),
        ],
        compiler_params=pltpu.CompilerParams(
            dimension_semantics=("arbitrary", "arbitrary"), vmem_limit_bytes=VMEM_LIMIT),
        name="diff_attn",
    )(lam_params_l, sg, q, k, vt, k, vt)
    if not with_ctx_queries:
        return attn, None
    attn_ctx = pl.pallas_call(
        functools.partial(_attn_ctx_kernel, lam_init=lam_init),
        grid=(BATCH, N_HEADS),
        in_specs=[
            pl.BlockSpec((4, HEAD_DIM), lambda b, h: (0, 0)),
            pl.BlockSpec((V_DIM, 1), lambda b, h: (0, 0)),
            pl.BlockSpec((CTX_LEN, V_DIM), lambda b, h: (ctx_blk0 + b, h)),
            pl.BlockSpec((CTX_LEN, V_DIM), lambda b, h: (ctx_blk0 + b, h)),
            pl.BlockSpec((V_DIM, CTX_LEN), lambda b, h: (h, ctx_blk0 + b)),
        ],
        out_specs=pl.BlockSpec((CTX_LEN, V_DIM), lambda b, h: (b, h)),
        out_shape=jax.ShapeDtypeStruct((BATCH * CTX_LEN, ATTN_WIDTH), BF16),
        compiler_params=pltpu.CompilerParams(
            dimension_semantics=("arbitrary", "arbitrary"), vmem_limit_bytes=VMEM_LIMIT),
        name="diff_attn_ctx",
    )(lam_params_l, sg, q, k, vt)
    return attn, attn_ctx


def _dft1_kernel(x_ref, w_ref, twr_ref, twi_ref, o_ref):
    n = DFT_A * NS2
    x = x_ref[...].reshape(n, 2 * FOURIER_WIDTH)
    xs = jnp.concatenate([x[:, :FOURIER_WIDTH], x[:, FOURIER_WIDTH:]], axis=0)
    r = _dot(w_ref[...], xs)
    re, im = r[:n], r[n:]
    twr, twi = twr_ref[0], twi_ref[0]
    o_ref[:, :, :FOURIER_WIDTH] = (re * twr - im * twi).astype(BF16).reshape(DFT_A, NS2, FOURIER_WIDTH)
    o_ref[:, :, FOURIER_WIDTH:] = (re * twi + im * twr).astype(BF16).reshape(DFT_A, NS2, FOURIER_WIDTH)


def _dft2_kernel(x_ref, w_ref, o_ref, *, n, nk, scale):
    for j in range(nk):
        xs = jnp.concatenate([x_ref[j * n:(j + 1) * n, :FOURIER_WIDTH],
                              x_ref[j * n:(j + 1) * n, FOURIER_WIDTH:]], axis=0)
        r = _dot(w_ref[...], xs) * scale
        o_ref[:, j * FOURIER_WIDTH:(j + 1) * FOURIER_WIDTH] = r.astype(BF16)


def _fourier_latent(f_all, t1, w2):
    w1, twr, twi = t1
    blk = (DFT_A, NS2, 2 * FOURIER_WIDTH)
    n = DFT_A * NS2
    x2 = pl.pallas_call(
        _dft1_kernel,
        grid=(BATCH, DFT_B // NS2),
        in_specs=[
            pl.BlockSpec(blk, lambda b, j: (b, j, 0)),
            pl.BlockSpec((2 * n, 2 * n), lambda b, j: (0, 0)),
            pl.BlockSpec((1, n, 1), lambda b, j: (j, 0, 0)),
            pl.BlockSpec((1, n, 1), lambda b, j: (j, 0, 0)),
        ],
        out_specs=pl.BlockSpec(blk, lambda b, j: (b, j, 0)),
        out_shape=jax.ShapeDtypeStruct((BATCH * DFT_A, DFT_B, 2 * FOURIER_WIDTH), BF16),
        compiler_params=pltpu.CompilerParams(
            dimension_semantics=("arbitrary", "arbitrary"), vmem_limit_bytes=VMEM_LIMIT),
        name="dft_stage1",
    )(f_all.reshape(R_ALL // DFT_B, DFT_B, 2 * FOURIER_WIDTH), w1, twr, twi)
    rows = R_LAT
    y = pl.pallas_call(
        functools.partial(_dft2_kernel, n=DFT_B, nk=NK1, scale=(SEQ * FGROUP_DIM) ** -0.5),
        grid=(BATCH, DFT_A // NK1),
        in_specs=[
            pl.BlockSpec((NK1 * DFT_B, 2 * FOURIER_WIDTH), lambda b, j: (b * (DFT_A // NK1) + j, 0)),
            pl.BlockSpec((DFT_B, 2 * DFT_B), lambda b, j: (0, 0)),
        ],
        out_specs=pl.BlockSpec((DFT_B, NK1 * FOURIER_WIDTH), lambda b, j: (b, j)),
        out_shape=jax.ShapeDtypeStruct((rows // DFT_A, DFT_A * FOURIER_WIDTH), BF16),
        compiler_params=pltpu.CompilerParams(
            dimension_semantics=("arbitrary", "arbitrary"), vmem_limit_bytes=VMEM_LIMIT),
        name="dft_stage2",
    )(x2.reshape(R_LAT, 2 * FOURIER_WIDTH), w2)
    return y.reshape(rows, FOURIER_WIDTH)


def _fourier_ctx(f_all, wctx):
    ctx_blk0 = R_LAT // CTX_LEN
    return pl.pallas_call(
        functools.partial(_dft2_kernel, n=CTX_LEN, nk=1, scale=(CTX_LEN * FGROUP_DIM) ** -0.5),
        grid=(BATCH,),
        in_specs=[
            pl.BlockSpec((CTX_LEN, 2 * FOURIER_WIDTH), lambda b: (ctx_blk0 + b, 0)),
            pl.BlockSpec((CTX_LEN, 2 * CTX_LEN), lambda b: (0, 0)),
        ],
        out_specs=pl.BlockSpec((CTX_LEN, FOURIER_WIDTH), lambda b: (b, 0)),
        out_shape=jax.ShapeDtypeStruct((BATCH * CTX_LEN, FOURIER_WIDTH), BF16),
        compiler_params=pltpu.CompilerParams(
            dimension_semantics=("arbitrary",), vmem_limit_bytes=VMEM_LIMIT),
        name="dft_ctx",
    )(f_all, wctx)


def _merge_kernel(*refs, split_input):
    refs = list(refs)
    x_ref, m_ref, a_ref, f_ref = refs[:4]
    ac_ref, fc_ref = (refs[4], refs[5]) if split_input else (None, None)
    ga_ref, gf_ref, wa_ref, wf_ref, wo_ref, o_ref = refs[-6:]
    attn, four = a_ref[...], f_ref[...]
    if split_input:
        latent = pl.program_id(0) < N_LAT_TILES
        attn = jnp.where(latent, attn, ac_ref[...])
        four = jnp.where(latent, four, fc_ref[...])
    a = _dot(attn, wa_ref[0])
    fr = _dot(four, wf_ref[0])
    merged = ga_ref[...].astype(F32) * a + gf_ref[...].astype(F32) * fr
    mix = _dot(merged.astype(BF16), wo_ref[0])
    o_ref[...] = x_ref[...] + m_ref[0, 0][5:6] * mix


def _merge(x, m_l, attn, four, ga, gf, wa, wf, wo, *, layer, n_tiles, attn_ctx=None, four_ctx=None):
    split_input = attn_ctx is not None
    const = dict(pipeline_mode=pl.Buffered(1))
    row = lambda i: (i, 0)
    lat_row = (lambda i: (jnp.minimum(i, N_LAT_TILES - 1), 0)) if split_input else row
    fix = lambda i: (0, 0)
    wsel = lambda i: (layer, 0, 0)
    in_specs = [
        pl.BlockSpec((TM, D_MODEL), row),
        pl.BlockSpec((1, 1, N_MOD, D_MODEL), lambda i: (0, _mod_row(i), 0, 0)),
        pl.BlockSpec((TM, ATTN_WIDTH), lat_row),
        pl.BlockSpec((TM, FOURIER_WIDTH), lat_row),
    ]
    args = [x, m_l, attn, four]
    if split_input:
        in_specs += [pl.BlockSpec((TM, ATTN_WIDTH), fix), pl.BlockSpec((TM, FOURIER_WIDTH), fix)]
        args += [attn_ctx, four_ctx]
    in_specs += [
        pl.BlockSpec((TM, D_MODEL), row),
        pl.BlockSpec((TM, D_MODEL), row),
        pl.BlockSpec((1, ATTN_WIDTH, D_MODEL), wsel, **const),
        pl.BlockSpec((1, FOURIER_WIDTH, D_MODEL), wsel, **const),
        pl.BlockSpec((1, D_MODEL, D_MODEL), wsel, **const),
    ]
    args += [ga, gf, wa, wf, wo]
    return pl.pallas_call(
        functools.partial(_merge_kernel, split_input=split_input),
        grid=(n_tiles,),
        in_specs=in_specs,
        out_specs=pl.BlockSpec((TM, D_MODEL), row),
        out_shape=jax.ShapeDtypeStruct((n_tiles * TM, D_MODEL), F32),
        compiler_params=pltpu.CompilerParams(
            dimension_semantics=("arbitrary",), vmem_limit_bytes=VMEM_LIMIT),
        name="merge",
    )(*args)


def _rope_tables():
    s = jnp.arange(SEQ, dtype=jnp.int32)
    lane = jnp.arange(128, dtype=jnp.int32) % HEAD_DIM
    inv = 1.0 / (ROPE_BASE ** ((2 * (lane % (AXIS_DIM // 2))).astype(F32) / AXIS_DIM))
    pos = jnp.where(lane[None, :] < AXIS_DIM, (s // GRID_W)[:, None], (s % GRID_W)[:, None]).astype(F32)
    ang = pos * inv[None, :]
    cos, sin = jnp.cos(ang), jnp.sin(ang)
    first = (lane % AXIS_DIM) < (AXIS_DIM // 2)
    sa = jnp.where(first[None, :], -sin, 0.0)
    sb = jnp.where(first[None, :], 0.0, sin)
    n_ctx = BATCH * CTX_LEN

    def full(t, fill):
        return jnp.concatenate([t] * BATCH + [jnp.full((n_ctx, 128), fill, F32)], axis=0)

    return full(cos, 1.0), full(sa, 0.0), full(sb, 0.0)


def _angles(num, den):
    return (2.0 * math.pi / den) * (num % den).astype(F32)


def _dft_tables():
    c = jnp.arange(FGROUP_DIM, dtype=jnp.int32)
    ang = _angles(c[:, None] * c[None, :], FGROUP_DIM)
    wc = jnp.concatenate([jnp.cos(ang), -jnp.sin(ang)], axis=1).astype(BF16)

    a = jnp.arange(DFT_A, dtype=jnp.int32)
    ang = _angles(a[:, None] * a[None, :], DFT_A)
    cs, sn = jnp.cos(ang), jnp.sin(ang)
    m4 = jnp.stack([jnp.stack([cs, sn]), jnp.stack([-sn, cs])])
    n = DFT_A * NS2
    w1 = jnp.einsum("pqks,jJ->pkjqsJ", m4, jnp.eye(NS2, dtype=F32)).reshape(2 * n, 2 * n).astype(BF16)
    s2 = jnp.arange(DFT_B, dtype=jnp.int32).reshape(DFT_B // NS2, 1, NS2)
    ang = _angles(a[None, :, None] * s2, SEQ).reshape(DFT_B // NS2, n, 1)
    t1 = (w1, jnp.cos(ang), -jnp.sin(ang))

    b = jnp.arange(DFT_B, dtype=jnp.int32)
    ang = _angles(b[:, None] * b[None, :], DFT_B)
    w2 = jnp.concatenate([jnp.cos(ang), jnp.sin(ang)], axis=1).astype(BF16)

    n = jnp.arange(CTX_LEN, dtype=jnp.int32)
    ang = _angles(n[:, None] * n[None, :], CTX_LEN)
    wctx = jnp.concatenate([jnp.cos(ang), jnp.sin(ang)], axis=1).astype(BF16)
    return wc, t1, w2, wctx


def kernel(x, c, ctx, c_ctx, w_mod, b_mod, norm_g, ffn_w_in, ffn_w_out, w_in, b_gate,
           lam_params, subln_g, w_attn_out, w_four_out, w_o, final_g):
    xs = x.reshape(R_LAT, D_MODEL)
    x_ctx = ctx.reshape(BATCH * CTX_LEN, D_MODEL)
    conds = jnp.concatenate([c, c_ctx[None, :], jnp.zeros((8 - BATCH - 1, D_MODEL), F32)], axis=0)
    m_all = _modulation(conds, w_mod, b_mod)
    tabs = _rope_tables()
    wc, t1, w2, wctx = _dft_tables()

    w_ffn_in, w_ffn_out = ffn_w_in.astype(BF16), ffn_w_out.astype(BF16)
    w_in_b = w_in.astype(BF16)
    w_v_t = w_in[:, :, 2 * QK_W:F_OFF].transpose(0, 2, 1).astype(BF16)
    wa, wf, wo = w_attn_out.astype(BF16), w_four_out.astype(BF16), w_o.astype(BF16)

    for l in range(DEPTH):
        last = l == DEPTH - 1
        lam_init = 0.8 - 0.6 * math.exp(-0.3 * l)
        m_l = m_all[l:l + 1]
        xs = _ffn(xs, m_l, norm_g[l, 0], w_ffn_in, w_ffn_out, wsel=(l, 0), sub=0, n_tiles=N_ALL_TILES,
                  x_ctx=x_ctx if l == 0 else None)
        q, k, vt, f, ga, gf = _proj(xs, m_l, norm_g[l, 1], w_in_b, w_v_t, tabs, wc, b_gate[l], layer=l)
        attn, attn_ctx = _attention(q, k, vt, lam_params[l], subln_g[l], lam_init=lam_init,
                                    with_ctx_queries=not last)
        four = _fourier_latent(f, t1, w2)
        four_ctx = None if last else _fourier_ctx(f, wctx)
        n_tiles = N_LAT_TILES if last else N_ALL_TILES
        xs = _merge(xs, m_l, attn, four, ga, gf, wa, wf, wo, layer=l, n_tiles=n_tiles,
                    attn_ctx=attn_ctx, four_ctx=four_ctx)
        xs = _ffn(xs, m_l, norm_g[l, 2], w_ffn_in, w_ffn_out, wsel=(l, 1), sub=2, n_tiles=n_tiles,
                  final_g=final_g if last else None)
    return xs.reshape(BATCH, SEQ, D_MODEL)
```

```python
import functools
import math

import jax
import jax.numpy as jnp
from jax import lax
from jax.experimental import pallas as pl
from jax.experimental.pallas import tpu as pltpu

D_MODEL = 1024
BATCH = 2
SEQ = 8192
DEPTH = 2
CTX_LEN = 256
GRID_W = 64
N_HEADS = 8
HEAD_DIM = 64
V_DIM = 2 * HEAD_DIM
QK_W = N_HEADS * 2 * HEAD_DIM
ATTN_WIDTH = N_HEADS * V_DIM
N_FGROUPS = 4
FGROUP_DIM = 128
FOURIER_WIDTH = N_FGROUPS * FGROUP_DIM
F_OFF = 2 * QK_W + ATTN_WIDTH
G_OFF = F_OFF + FOURIER_WIDTH
IN_WIDTH = G_OFF + 2 * D_MODEL
D_FF = 2816
AXIS_DIM = HEAD_DIM // 2
ROPE_BASE = 10000.0
N_MOD = 9
EPS = 1e-6

R_LAT = BATCH * SEQ
R_ALL = R_LAT + BATCH * CTX_LEN
TM = 512
N_LAT_TILES = R_LAT // TM
N_ALL_TILES = R_ALL // TM
FF_CHUNK = 256
N_FF_CHUNKS = D_FF // FF_CHUNK
TQ = 256
TK = 1024
N_LAT_BLOCKS = SEQ // TK
PROB_SUM_LIMIT = 2.0 ** 24
N_LAT_QT = SEQ // TQ
DFT_A = 64
DFT_B = 128
NS2 = 16
NK1 = 8
VMEM_LIMIT = 56 * 1024 * 1024

F32 = jnp.float32
BF16 = jnp.bfloat16


def _dot(a, b):
    return jnp.dot(a, b, preferred_element_type=F32)


def _sigmoid(x):
    return 1.0 / (1.0 + jnp.exp(-x))


def _mod_row(i):
    return jnp.minimum(i // (SEQ // TM), 2)


def _norm_mod(x, g, shift, scale):
    var = jnp.mean(x * x, axis=-1, keepdims=True)
    y = x * lax.rsqrt(var + EPS) * g
    return y * (1.0 + scale) + shift


def _mod_kernel(c_ref, w_ref, b_ref, o_ref):
    c = c_ref[...]
    s = c * _sigmoid(c)
    o_ref[0] = jnp.dot(s, w_ref[0], preferred_element_type=F32,
                       precision=lax.Precision.HIGHEST) + b_ref[0]


def _modulation(conds, w_mod, b_mod):
    tn = D_MODEL
    out = pl.pallas_call(
        _mod_kernel,
        grid=(DEPTH, N_MOD),
        in_specs=[
            pl.BlockSpec((8, D_MODEL), lambda l, j: (0, 0)),
            pl.BlockSpec((1, D_MODEL, tn), lambda l, j: (l, 0, j)),
            pl.BlockSpec((1, 1, tn), lambda l, j: (l, 0, j)),
        ],
        out_specs=pl.BlockSpec((1, 8, tn), lambda l, j: (l, 0, j)),
        out_shape=jax.ShapeDtypeStruct((DEPTH, 8, N_MOD * D_MODEL), F32),
        compiler_params=pltpu.CompilerParams(
            dimension_semantics=("arbitrary", "arbitrary"), vmem_limit_bytes=VMEM_LIMIT),
        name="modulation",
    )(conds, w_mod, b_mod.reshape(DEPTH, 1, N_MOD * D_MODEL))
    return out.reshape(DEPTH, 8, N_MOD, D_MODEL)


def _ffn_kernel(*refs, sub, final, split_input):
    refs = list(refs)
    x_ref = refs.pop(0)
    xc_ref = refs.pop(0) if split_input else None
    m_ref, g_ref, win_ref, wout_ref = refs[:4]
    fg_ref = refs[4] if final else None
    o_ref, u_ref, h_ref = refs[-3:]
    x = x_ref[...]
    if split_input:
        x = jnp.where(pl.program_id(0) < N_LAT_TILES, x, xc_ref[...])
    m = m_ref[0, 0]
    shift = m[3 * sub:3 * sub + 1]
    scale = m[3 * sub + 1:3 * sub + 2]
    gate = m[3 * sub + 2:3 * sub + 3]
    u_ref[...] = _norm_mod(x, g_ref[...], shift, scale).astype(BF16)
    for c in range(N_FF_CHUNKS):
        lo = c * FF_CHUNK
        a = _dot(u_ref[...], win_ref[0, 0, :, lo:lo + FF_CHUNK])
        b = _dot(u_ref[...], win_ref[0, 0, :, D_FF + lo:D_FF + lo + FF_CHUNK])
        h_ref[:, lo:lo + FF_CHUNK] = (a * _sigmoid(a) * b).astype(BF16)
    y = x + 0.5 * gate * _dot(h_ref[...], wout_ref[0, 0])
    if final:
        var = jnp.mean(y * y, axis=-1, keepdims=True)
        y = y * lax.rsqrt(var + EPS) * fg_ref[...]
    o_ref[...] = y


def _ffn(x, m_l, g, w_in_c, w_out_c, *, wsel, sub, n_tiles, final_g=None, x_ctx=None):
    final = final_g is not None
    split_input = x_ctx is not None
    const = dict(pipeline_mode=pl.Buffered(1))
    if split_input:
        in_specs = [pl.BlockSpec((TM, D_MODEL), lambda i: (jnp.minimum(i, N_LAT_TILES - 1), 0)),
                    pl.BlockSpec((TM, D_MODEL), lambda i: (0, 0))]
        args = [x, x_ctx]
    else:
        in_specs = [pl.BlockSpec((TM, D_MODEL), lambda i: (i, 0))]
        args = [x]
    in_specs += [
        pl.BlockSpec((1, 1, N_MOD, D_MODEL), lambda i: (0, _mod_row(i), 0, 0)),
        pl.BlockSpec((1, D_MODEL), lambda i: (0, 0)),
        pl.BlockSpec((1, 1, D_MODEL, 2 * D_FF), lambda i: wsel + (0, 0), **const),
        pl.BlockSpec((1, 1, D_FF, D_MODEL), lambda i: wsel + (0, 0), **const),
    ]
    args += [m_l, g.reshape(1, D_MODEL), w_in_c, w_out_c]
    if final:
        in_specs.append(pl.BlockSpec((1, D_MODEL), lambda i: (0, 0)))
        args.append(final_g.reshape(1, D_MODEL))
    return pl.pallas_call(
        functools.partial(_ffn_kernel, sub=sub, final=final, split_input=split_input),
        grid=(n_tiles,),
        in_specs=in_specs,
        out_specs=pl.BlockSpec((TM, D_MODEL), lambda i: (i, 0)),
        out_shape=jax.ShapeDtypeStruct((n_tiles * TM, D_MODEL), F32),
        scratch_shapes=[pltpu.VMEM((TM, D_MODEL), BF16), pltpu.VMEM((TM, D_FF), BF16)],
        compiler_params=pltpu.CompilerParams(
            dimension_semantics=("arbitrary",), vmem_limit_bytes=VMEM_LIMIT),
        name="ffn",
    )(*args)


def _proj_kernel(x_ref, m_ref, g_ref, w_ref, wvt_ref, cos_ref, sa_ref, sb_ref, wc_ref, bg_ref,
                 q_ref, k_ref, vt_ref, f_ref, ga_ref, gf_ref, u_ref):
    x = x_ref[...]
    m = m_ref[0, 0]
    u_ref[...] = _norm_mod(x, g_ref[...], m[3:4], m[4:5]).astype(BF16)
    cw = 256
    cos = jnp.concatenate([cos_ref[...]] * (cw // 128), axis=1)
    sa = jnp.concatenate([sa_ref[...]] * (cw // 128), axis=1)
    sb = jnp.concatenate([sb_ref[...]] * (cw // 128), axis=1)

    def col(off):
        return _dot(u_ref[...], w_ref[0, :, off:off + cw])

    for out_ref, base, qscale in ((q_ref, 0, HEAD_DIM ** -0.5 * math.log2(math.e)), (k_ref, QK_W, 1.0)):
        for j in range(QK_W // cw):
            p = col(base + j * cw)
            r = p * cos + pltpu.roll(p, cw - 16, 1) * sa + pltpu.roll(p, 16, 1) * sb
            if qscale != 1.0:
                r = r * qscale
            out_ref[:, j * cw:(j + 1) * cw] = r.astype(BF16)
    for j in range(ATTN_WIDTH // cw):
        vt = lax.dot_general(wvt_ref[0, j * cw:(j + 1) * cw, :], u_ref[...], (((1,), (1,)), ((), ())),
                             preferred_element_type=F32)
        vt_ref[j * cw:(j + 1) * cw, :] = vt.astype(BF16)
    for j in range(FOURIER_WIDTH // cw):
        p = col(F_OFF + j * cw).astype(BF16)
        for t in range(cw // FGROUP_DIM):
            grp = j * (cw // FGROUP_DIM) + t
            r = _dot(p[:, t * FGROUP_DIM:(t + 1) * FGROUP_DIM], wc_ref[...])
            shape3 = (TM // DFT_B, DFT_B, FGROUP_DIM)
            f_ref[:, :, grp * FGROUP_DIM:(grp + 1) * FGROUP_DIM] = r[:, :FGROUP_DIM].astype(BF16).reshape(shape3)
            f_ref[:, :, FOURIER_WIDTH + grp * FGROUP_DIM:FOURIER_WIDTH + (grp + 1) * FGROUP_DIM] = (
                r[:, FGROUP_DIM:].astype(BF16).reshape(shape3))
    for out_ref, base, brow in ((ga_ref, G_OFF, 0), (gf_ref, G_OFF + D_MODEL, 1)):
        for j in range(D_MODEL // cw):
            p = col(base + j * cw) + bg_ref[brow:brow + 1, j * cw:(j + 1) * cw]
            out_ref[:, j * cw:(j + 1) * cw] = _sigmoid(p).astype(BF16)


def _proj(x, m_l, g, w_in_b, w_v_t, tabs, wc, b_gate, *, layer):
    cos, sa, sb = tabs
    const = dict(pipeline_mode=pl.Buffered(1))
    row = lambda i: (i, 0)
    fix = lambda i: (0, 0)
    wide = jax.ShapeDtypeStruct((R_ALL, D_MODEL), BF16)
    row_spec = pl.BlockSpec((TM, D_MODEL), row)
    return pl.pallas_call(
        _proj_kernel,
        grid=(N_ALL_TILES,),
        in_specs=[
            pl.BlockSpec((TM, D_MODEL), row),
            pl.BlockSpec((1, 1, N_MOD, D_MODEL), lambda i: (0, _mod_row(i), 0, 0)),
            pl.BlockSpec((1, D_MODEL), fix),
            pl.BlockSpec((1, D_MODEL, IN_WIDTH), lambda i: (layer, 0, 0), **const),
            pl.BlockSpec((1, ATTN_WIDTH, D_MODEL), lambda i: (layer, 0, 0), **const),
            pl.BlockSpec((TM, 128), row),
            pl.BlockSpec((TM, 128), row),
            pl.BlockSpec((TM, 128), row),
            pl.BlockSpec((FGROUP_DIM, 2 * FGROUP_DIM), fix),
            pl.BlockSpec((2, D_MODEL), fix),
        ],
        out_specs=[row_spec, row_spec, pl.BlockSpec((ATTN_WIDTH, TM), lambda i: (0, i)),
                   pl.BlockSpec((TM // DFT_B, DFT_B, 2 * FOURIER_WIDTH), lambda i: (i, 0, 0)), row_spec, row_spec],
        out_shape=[wide, wide, jax.ShapeDtypeStruct((ATTN_WIDTH, R_ALL), BF16),
                   jax.ShapeDtypeStruct((R_ALL // DFT_B, DFT_B, 2 * FOURIER_WIDTH), BF16), wide, wide],
        scratch_shapes=[pltpu.VMEM((TM, D_MODEL), BF16)],
        compiler_params=pltpu.CompilerParams(
            dimension_semantics=("arbitrary",), vmem_limit_bytes=VMEM_LIMIT),
        name="proj",
    )(x, m_l, g.reshape(1, D_MODEL), w_in_b, w_v_t, cos, sa, sb, wc, b_gate)


def _masked_queries(q):
    lane = lax.broadcasted_iota(jnp.int32, q.shape, 1)
    zero = jnp.zeros_like(q)
    return jnp.where(lane < HEAD_DIM, q, zero), jnp.where(lane >= HEAD_DIM, q, zero)


def _lambda(lp, lam_init):
    return (jnp.exp(jnp.sum(lp[0:1] * lp[1:2], axis=-1, keepdims=True))
            - jnp.exp(jnp.sum(lp[2:3] * lp[3:4], axis=-1, keepdims=True)) + lam_init)


def _attn_epilogue(acc, l, lam, sg, lam_init):
    n = acc.shape[1] // 2
    o = acc / l
    d = o[:, :n] - lam * o[:, n:]
    var = jnp.mean(d * d, axis=0, keepdims=True)
    y = d * lax.rsqrt(var + EPS) * sg * (1.0 - lam_init)
    return y.T.astype(BF16)


def _attn_kernel(lp_ref, sg_ref, q_ref, kc_ref, vc_ref, kl_ref, vl_ref, o_ref,
                 qs_ref, pacc_ref, pl_ref, qx_ref, m_ref, l_ref, acc_ref, *, lam_init):
    lam = _lambda(lp_ref[...], lam_init)

    def tile_rows(t):
        if isinstance(t, int):
            return pl.ds(t * TQ, TQ)
        return pl.ds(pl.multiple_of(t * TQ, TQ), TQ)

    def stage_queries(dst, t):
        q0, q1 = _masked_queries(q_ref[tile_rows(t), :])
        dst[0:TQ, :] = q0
        dst[TQ:, :] = q1

    def scores(keys, queries_ref):
        return lax.dot_general(keys, queries_ref[...], (((1,), (1,)), ((), ())), preferred_element_type=F32)

    def fold8(fn, a):
        return fn(a.reshape(a.shape[0] // 8, 8, 2 * TQ), axis=0)

    def latent_block(blk):
        return kl_ref[blk * TK:(blk + 1) * TK, :], vl_ref[:, blk * TK:(blk + 1) * TK]

    def store_tile(t, acc, l8):
        l = jnp.sum(l8, axis=0, keepdims=True)
        o_ref[tile_rows(t), :] = _attn_epilogue(acc, l, lam, sg_ref[...], lam_init)

    def exact_tile(t):
        stage_queries(qx_ref, t)
        m_ref[...] = jnp.full_like(m_ref, -jnp.inf)
        l_ref[...] = jnp.zeros_like(l_ref)
        acc_ref[...] = jnp.zeros_like(acc_ref)

        def update(keys, vt):
            s = scores(keys, qx_ref)
            m_old = m_ref[...]
            m_new = jnp.maximum(m_old, jnp.max(s, axis=0, keepdims=True))
            alpha = jnp.exp2(m_old - m_new)
            p = jnp.exp2(s - m_new)
            m_ref[...] = m_new
            l_ref[...] = alpha * l_ref[...] + fold8(jnp.sum, p)
            acc_ref[...] = alpha * acc_ref[...] + _dot(vt, p.astype(BF16))

        def latent(blk, carry):
            off = pl.multiple_of(blk * TK, TK)
            update(kl_ref[pl.ds(off, TK), :], vl_ref[:, pl.ds(off, TK)])
            return carry

        update(kc_ref[...], vc_ref[...])
        lax.fori_loop(0, N_LAT_BLOCKS, latent, 0)
        store_tile(t, acc_ref[...], l_ref[...])

    def tile(t, redo_prev):
        stage_queries(qs_ref, t)
        s = scores(kc_ref[...], qs_ref)
        m0 = jnp.max(s, axis=0, keepdims=True)
        p = jnp.exp2(s - m0)
        l8 = fold8(jnp.sum, p)
        acc = _dot(vc_ref[...], p.astype(BF16))
        for blk in range(N_LAT_BLOCKS):
            keys, vt = latent_block(blk)
            p = jnp.exp2(scores(keys, qs_ref) - m0)
            l8 = l8 + fold8(jnp.sum, p)
            acc = acc + _dot(vt, p.astype(BF16))
        prev = jnp.maximum(t - 1, 0)
        store_tile(prev, pacc_ref[...], pl_ref[...])
        pacc_ref[...] = acc
        pl_ref[...] = l8
        redo = (jnp.max(l8) > PROB_SUM_LIMIT).astype(jnp.int32)

        @pl.when(redo_prev == 1)
        def _():
            exact_tile(prev)

        return redo

    pacc_ref[...] = jnp.ones_like(pacc_ref)
    pl_ref[...] = jnp.ones_like(pl_ref)
    redo_last = lax.fori_loop(0, N_LAT_QT, tile, jnp.int32(0))
    store_tile(N_LAT_QT - 1, pacc_ref[...], pl_ref[...])

    @pl.when(redo_last == 1)
    def _():
        exact_tile(N_LAT_QT - 1)


def _attn_ctx_kernel(lp_ref, sg_ref, q_ref, k_ref, vt_ref, o_ref, *, lam_init):
    q0, q1 = _masked_queries(q_ref[...])
    qs = jnp.concatenate([q0, q1], axis=0)
    s = lax.dot_general(k_ref[...], qs, (((1,), (1,)), ((), ())), preferred_element_type=F32)
    p = jnp.exp2(s - jnp.max(s, axis=0, keepdims=True))
    l = jnp.sum(p, axis=0, keepdims=True)
    acc = _dot(vt_ref[...], p.astype(BF16))
    o_ref[...] = _attn_epilogue(acc, l, _lambda(lp_ref[...], lam_init), sg_ref[...], lam_init)


def _attention(q, k, vt, lam_params_l, subln_g_l, *, lam_init, with_ctx_queries):
    ctx_blk0 = R_LAT // CTX_LEN
    sg = subln_g_l.reshape(V_DIM, 1)
    attn = pl.pallas_call(
        functools.partial(_attn_kernel, lam_init=lam_init),
        grid=(BATCH, N_HEADS),
        in_specs=[
            pl.BlockSpec((4, HEAD_DIM), lambda b, h: (0, 0)),
            pl.BlockSpec((V_DIM, 1), lambda b, h: (0, 0)),
            pl.BlockSpec((SEQ, V_DIM), lambda b, h: (b, h)),
            pl.BlockSpec((CTX_LEN, V_DIM), lambda b, h: (ctx_blk0 + b, h)),
            pl.BlockSpec((V_DIM, CTX_LEN), lambda b, h: (h, ctx_blk0 + b)),
            pl.BlockSpec((SEQ, V_DIM), lambda b, h: (b, h)),
            pl.BlockSpec((V_DIM, SEQ), lambda b, h: (h, b)),
        ],
        out_specs=pl.BlockSpec((SEQ, V_DIM), lambda b, h: (b, h)),
        out_shape=jax.ShapeDtypeStruct((R_LAT, ATTN_WIDTH), BF16),
        scratch_shapes=[
            pltpu.VMEM((2 * TQ, V_DIM), BF16),
            pltpu.VMEM((V_DIM, 2 * TQ), F32),
            pltpu.VMEM((8, 2 * TQ), F32),
            pltpu.VMEM((2 * TQ, V_DIM), BF16),
            pltpu.VMEM((1, 2 * TQ), F32),
            pltpu.VMEM((8, 2 * TQ), F32),
            pltpu.VMEM((V_DIM, 2 * TQ), F32),
        ],
        compiler_params=pltpu.CompilerParams(
            dimension_semantics=("arbitrary", "arbitrary"), vmem_limit_bytes=VMEM_LIMIT),
        name="diff_attn",
    )(lam_params_l, sg, q, k, vt, k, vt)
    if not with_ctx_queries:
        return attn, None
    attn_ctx = pl.pallas_call(
        functools.partial(_attn_ctx_kernel, lam_init=lam_init),
        grid=(BATCH, N_HEADS),
        in_specs=[
            pl.BlockSpec((4, HEAD_DIM), lambda b, h: (0, 0)),
            pl.BlockSpec((V_DIM, 1), lambda b, h: (0, 0)),
            pl.BlockSpec((CTX_LEN, V_DIM), lambda b, h: (ctx_blk0 + b, h)),
            pl.BlockSpec((CTX_LEN, V_DIM), lambda b, h: (ctx_blk0 + b, h)),
            pl.BlockSpec((V_DIM, CTX_LEN), lambda b, h: (h, ctx_blk0 + b)),
        ],
        out_specs=pl.BlockSpec((CTX_LEN, V_DIM), lambda b, h: (b, h)),
        out_shape=jax.ShapeDtypeStruct((BATCH * CTX_LEN, ATTN_WIDTH), BF16),
        compiler_params=pltpu.CompilerParams(
            dimension_semantics=("arbitrary", "arbitrary"), vmem_limit_bytes=VMEM_LIMIT),
        name="diff_attn_ctx",
    )(lam_params_l, sg, q, k, vt)
    return attn, attn_ctx


def _dft1_kernel(x_ref, w_ref, twr_ref, twi_ref, o_ref):
    n = DFT_A * NS2
    x = x_ref[...].reshape(n, 2 * FOURIER_WIDTH)
    xs = jnp.concatenate([x[:, :FOURIER_WIDTH], x[:, FOURIER_WIDTH:]], axis=0)
    r = _dot(w_ref[...], xs)
    re, im = r[:n], r[n:]
    twr, twi = twr_ref[0], twi_ref[0]
    o_ref[:, :, :FOURIER_WIDTH] = (re * twr - im * twi).astype(BF16).reshape(DFT_A, NS2, FOURIER_WIDTH)
    o_ref[:, :, FOURIER_WIDTH:] = (re * twi + im * twr).astype(BF16).reshape(DFT_A, NS2, FOURIER_WIDTH)


def _dft2_kernel(x_ref, w_ref, o_ref, *, n, nk, scale):
    g = n // DFT_B
    for j in range(nk):
        x = x_ref[j * g:(j + 1) * g].reshape(n, 2 * FOURIER_WIDTH)
        xs = jnp.concatenate([x[:, :FOURIER_WIDTH], x[:, FOURIER_WIDTH:]], axis=0)
        r = _dot(w_ref[...], xs) * scale
        o_ref[:, j * FOURIER_WIDTH:(j + 1) * FOURIER_WIDTH] = r.astype(BF16)


def _fourier_latent(f_all, t1, w2):
    w1, twr, twi = t1
    blk = (DFT_A, NS2, 2 * FOURIER_WIDTH)
    n = DFT_A * NS2
    x2 = pl.pallas_call(
        _dft1_kernel,
        grid=(BATCH, DFT_B // NS2),
        in_specs=[
            pl.BlockSpec(blk, lambda b, j: (b, j, 0)),
            pl.BlockSpec((2 * n, 2 * n), lambda b, j: (0, 0)),
            pl.BlockSpec((1, n, 1), lambda b, j: (j, 0, 0)),
            pl.BlockSpec((1, n, 1), lambda b, j: (j, 0, 0)),
        ],
        out_specs=pl.BlockSpec(blk, lambda b, j: (b, j, 0)),
        out_shape=jax.ShapeDtypeStruct((BATCH * DFT_A, DFT_B, 2 * FOURIER_WIDTH), BF16),
        compiler_params=pltpu.CompilerParams(
            dimension_semantics=("arbitrary", "arbitrary"), vmem_limit_bytes=VMEM_LIMIT),
        name="dft_stage1",
    )(f_all, w1, twr, twi)
    y = pl.pallas_call(
        functools.partial(_dft2_kernel, n=DFT_B, nk=NK1, scale=(SEQ * FGROUP_DIM) ** -0.5),
        grid=(BATCH, DFT_A // NK1),
        in_specs=[
            pl.BlockSpec((NK1, DFT_B, 2 * FOURIER_WIDTH), lambda b, j: (b * (DFT_A // NK1) + j, 0, 0)),
            pl.BlockSpec((DFT_B, 2 * DFT_B), lambda b, j: (0, 0)),
        ],
        out_specs=pl.BlockSpec((DFT_B, NK1 * FOURIER_WIDTH), lambda b, j: (b, j)),
        out_shape=jax.ShapeDtypeStruct((R_LAT // DFT_A, DFT_A * FOURIER_WIDTH), BF16),
        compiler_params=pltpu.CompilerParams(
            dimension_semantics=("arbitrary", "arbitrary"), vmem_limit_bytes=VMEM_LIMIT),
        name="dft_stage2",
    )(x2, w2)
    return y.reshape(R_LAT, FOURIER_WIDTH)


def _fourier_ctx(f_all, wctx):
    ctx_blk0 = R_LAT // CTX_LEN
    return pl.pallas_call(
        functools.partial(_dft2_kernel, n=CTX_LEN, nk=1, scale=(CTX_LEN * FGROUP_DIM) ** -0.5),
        grid=(BATCH,),
        in_specs=[
            pl.BlockSpec((CTX_LEN // DFT_B, DFT_B, 2 * FOURIER_WIDTH), lambda b: (ctx_blk0 + b, 0, 0)),
            pl.BlockSpec((CTX_LEN, 2 * CTX_LEN), lambda b: (0, 0)),
        ],
        out_specs=pl.BlockSpec((CTX_LEN, FOURIER_WIDTH), lambda b: (b, 0)),
        out_shape=jax.ShapeDtypeStruct((BATCH * CTX_LEN, FOURIER_WIDTH), BF16),
        compiler_params=pltpu.CompilerParams(
            dimension_semantics=("arbitrary",), vmem_limit_bytes=VMEM_LIMIT),
        name="dft_ctx",
    )(f_all, wctx)


def _merge_kernel(*refs, split_input):
    refs = list(refs)
    x_ref, m_ref, a_ref, f_ref = refs[:4]
    ac_ref, fc_ref = (refs[4], refs[5]) if split_input else (None, None)
    ga_ref, gf_ref, wa_ref, wf_ref, wo_ref, o_ref = refs[-6:]
    attn, four = a_ref[...], f_ref[...]
    if split_input:
        latent = pl.program_id(0) < N_LAT_TILES
        attn = jnp.where(latent, attn, ac_ref[...])
        four = jnp.where(latent, four, fc_ref[...])
    a = _dot(attn, wa_ref[0])
    fr = _dot(four, wf_ref[0])
    merged = ga_ref[...].astype(F32) * a + gf_ref[...].astype(F32) * fr
    mix = _dot(merged.astype(BF16), wo_ref[0])
    o_ref[...] = x_ref[...] + m_ref[0, 0][5:6] * mix


def _merge(x, m_l, attn, four, ga, gf, wa, wf, wo, *, layer, n_tiles, attn_ctx=None, four_ctx=None):
    split_input = attn_ctx is not None
    const = dict(pipeline_mode=pl.Buffered(1))
    row = lambda i: (i, 0)
    lat_row = (lambda i: (jnp.minimum(i, N_LAT_TILES - 1), 0)) if split_input else row
    fix = lambda i: (0, 0)
    wsel = lambda i: (layer, 0, 0)
    in_specs = [
        pl.BlockSpec((TM, D_MODEL), row),
        pl.BlockSpec((1, 1, N_MOD, D_MODEL), lambda i: (0, _mod_row(i), 0, 0)),
        pl.BlockSpec((TM, ATTN_WIDTH), lat_row),
        pl.BlockSpec((TM, FOURIER_WIDTH), lat_row),
    ]
    args = [x, m_l, attn, four]
    if split_input:
        in_specs += [pl.BlockSpec((TM, ATTN_WIDTH), fix), pl.BlockSpec((TM, FOURIER_WIDTH), fix)]
        args += [attn_ctx, four_ctx]
    in_specs += [
        pl.BlockSpec((TM, D_MODEL), row),
        pl.BlockSpec((TM, D_MODEL), row),
        pl.BlockSpec((1, ATTN_WIDTH, D_MODEL), wsel, **const),
        pl.BlockSpec((1, FOURIER_WIDTH, D_MODEL), wsel, **const),
        pl.BlockSpec((1, D_MODEL, D_MODEL), wsel, **const),
    ]
    args += [ga, gf, wa, wf, wo]
    return pl.pallas_call(
        functools.partial(_merge_kernel, split_input=split_input),
        grid=(n_tiles,),
        in_specs=in_specs,
        out_specs=pl.BlockSpec((TM, D_MODEL), row),
        out_shape=jax.ShapeDtypeStruct((n_tiles * TM, D_MODEL), F32),
        compiler_params=pltpu.CompilerParams(
            dimension_semantics=("arbitrary",), vmem_limit_bytes=VMEM_LIMIT),
        name="merge",
    )(*args)


def _rope_tables():
    s = jnp.arange(SEQ, dtype=jnp.int32)
    lane = jnp.arange(128, dtype=jnp.int32) % HEAD_DIM
    inv = 1.0 / (ROPE_BASE ** ((2 * (lane % (AXIS_DIM // 2))).astype(F32) / AXIS_DIM))
    pos = jnp.where(lane[None, :] < AXIS_DIM, (s // GRID_W)[:, None], (s % GRID_W)[:, None]).astype(F32)
    ang = pos * inv[None, :]
    cos, sin = jnp.cos(ang), jnp.sin(ang)
    first = (lane % AXIS_DIM) < (AXIS_DIM // 2)
    sa = jnp.where(first[None, :], -sin, 0.0)
    sb = jnp.where(first[None, :], 0.0, sin)
    n_ctx = BATCH * CTX_LEN

    def full(t, fill):
        return jnp.concatenate([t] * BATCH + [jnp.full((n_ctx, 128), fill, F32)], axis=0)

    return full(cos, 1.0), full(sa, 0.0), full(sb, 0.0)


def _angles(num, den):
    return (2.0 * math.pi / den) * (num % den).astype(F32)


def _dft_tables():
    c = jnp.arange(FGROUP_DIM, dtype=jnp.int32)
    ang = _angles(c[:, None] * c[None, :], FGROUP_DIM)
    wc = jnp.concatenate([jnp.cos(ang), -jnp.sin(ang)], axis=1).astype(BF16)

    n = DFT_A * NS2
    idx = jnp.arange(2 * n, dtype=jnp.int32)
    part, freq, pos = idx // n, (idx % n) // NS2, idx % NS2
    ang = _angles(freq[:, None] * freq[None, :], DFT_A)
    val = jnp.where(part[:, None] == part[None, :], jnp.cos(ang),
                    jnp.where(part[:, None] < part[None, :], jnp.sin(ang), -jnp.sin(ang)))
    w1 = jnp.where(pos[:, None] == pos[None, :], val, 0.0).astype(BF16)
    a = jnp.arange(DFT_A, dtype=jnp.int32)
    s2 = jnp.arange(DFT_B, dtype=jnp.int32).reshape(DFT_B // NS2, 1, NS2)
    ang = _angles(a[None, :, None] * s2, SEQ).reshape(DFT_B // NS2, n, 1)
    t1 = (w1, jnp.cos(ang), -jnp.sin(ang))

    b = jnp.arange(DFT_B, dtype=jnp.int32)
    ang = _angles(b[:, None] * b[None, :], DFT_B)
    w2 = jnp.concatenate([jnp.cos(ang), jnp.sin(ang)], axis=1).astype(BF16)

    n = jnp.arange(CTX_LEN, dtype=jnp.int32)
    ang = _angles(n[:, None] * n[None, :], CTX_LEN)
    wctx = jnp.concatenate([jnp.cos(ang), jnp.sin(ang)], axis=1).astype(BF16)
    return wc, t1, w2, wctx


def kernel(x, c, ctx, c_ctx, w_mod, b_mod, norm_g, ffn_w_in, ffn_w_out, w_in, b_gate,
           lam_params, subln_g, w_attn_out, w_four_out, w_o, final_g):
    xs = x.reshape(R_LAT, D_MODEL)
    x_ctx = ctx.reshape(BATCH * CTX_LEN, D_MODEL)
    conds = jnp.concatenate([c, c_ctx[None, :], jnp.zeros((8 - BATCH - 1, D_MODEL), F32)], axis=0)
    m_all = _modulation(conds, w_mod, b_mod)
    tabs = _rope_tables()
    wc, t1, w2, wctx = _dft_tables()

    w_ffn_in, w_ffn_out = ffn_w_in.astype(BF16), ffn_w_out.astype(BF16)
    w_in_b = w_in.astype(BF16)
    w_v_t = w_in[:, :, 2 * QK_W:F_OFF].transpose(0, 2, 1).astype(BF16)
    wa, wf, wo = w_attn_out.astype(BF16), w_four_out.astype(BF16), w_o.astype(BF16)

    for l in range(DEPTH):
        last = l == DEPTH - 1
        lam_init = 0.8 - 0.6 * math.exp(-0.3 * l)
        m_l = m_all[l:l + 1]
        xs = _ffn(xs, m_l, norm_g[l, 0], w_ffn_in, w_ffn_out, wsel=(l, 0), sub=0, n_tiles=N_ALL_TILES,
                  x_ctx=x_ctx if l == 0 else None)
        q, k, vt, f, ga, gf = _proj(xs, m_l, norm_g[l, 1], w_in_b, w_v_t, tabs, wc, b_gate[l], layer=l)
        attn, attn_ctx = _attention(q, k, vt, lam_params[l], subln_g[l], lam_init=lam_init,
                                    with_ctx_queries=not last)
        four = _fourier_latent(f, t1, w2)
        four_ctx = None if last else _fourier_ctx(f, wctx)
        n_tiles = N_LAT_TILES if last else N_ALL_TILES
        xs = _merge(xs, m_l, attn, four, ga, gf, wa, wf, wo, layer=l, n_tiles=n_tiles,
                    attn_ctx=attn_ctx, four_ctx=four_ctx)
        xs = _ffn(xs, m_l, norm_g[l, 2], w_ffn_in, w_ffn_out, wsel=(l, 1), sub=2, n_tiles=n_tiles,
                  final_g=final_g if last else None)
    return xs.reshape(BATCH, SEQ, D_MODEL)
```

```python
import functools
import math

import jax
import jax.numpy as jnp
from jax import lax
from jax.experimental import pallas as pl
from jax.experimental.pallas import tpu as pltpu

D_MODEL = 1024
BATCH = 2
SEQ = 8192
DEPTH = 2
CTX_LEN = 256
GRID_W = 64
N_HEADS = 8
HEAD_DIM = 64
V_DIM = 2 * HEAD_DIM
QK_W = N_HEADS * 2 * HEAD_DIM
ATTN_WIDTH = N_HEADS * V_DIM
N_FGROUPS = 4
FGROUP_DIM = 128
FOURIER_WIDTH = N_FGROUPS * FGROUP_DIM
F_OFF = 2 * QK_W + ATTN_WIDTH
G_OFF = F_OFF + FOURIER_WIDTH
IN_WIDTH = G_OFF + 2 * D_MODEL
D_FF = 2816
AXIS_DIM = HEAD_DIM // 2
ROPE_BASE = 10000.0
N_MOD = 9
EPS = 1e-6

R_LAT = BATCH * SEQ
R_ALL = R_LAT + BATCH * CTX_LEN
TM = 512
N_LAT_TILES = R_LAT // TM
N_ALL_TILES = R_ALL // TM
FF_CHUNK = 256
N_FF_CHUNKS = D_FF // FF_CHUNK
TQ = 256
TK = 1024
N_LAT_BLOCKS = SEQ // TK
PROB_SUM_LIMIT = 2.0 ** 24
N_LAT_QT = SEQ // TQ
DFT_A = 64
DFT_B = 128
NS2 = 16
NK1 = 8
VMEM_LIMIT = 56 * 1024 * 1024

F32 = jnp.float32
BF16 = jnp.bfloat16


def _dot(a, b):
    return jnp.dot(a, b, preferred_element_type=F32)


def _sigmoid(x):
    return 1.0 / (1.0 + jnp.exp(-x))


def _mod_row(i):
    return jnp.minimum(i // (SEQ // TM), 2)


def _norm_mod(x, g, shift, scale):
    var = jnp.mean(x * x, axis=-1, keepdims=True)
    y = x * lax.rsqrt(var + EPS) * g
    return y * (1.0 + scale) + shift


def _mod_kernel(c_ref, w_ref, b_ref, o_ref):
    c = c_ref[...]
    s = c * _sigmoid(c)
    o_ref[0] = jnp.dot(s, w_ref[0], preferred_element_type=F32,
                       precision=lax.Precision.HIGHEST) + b_ref[0]


def _modulation(conds, w_mod, b_mod):
    tn = D_MODEL
    out = pl.pallas_call(
        _mod_kernel,
        grid=(DEPTH, N_MOD),
        in_specs=[
            pl.BlockSpec((8, D_MODEL), lambda l, j: (0, 0)),
            pl.BlockSpec((1, D_MODEL, tn), lambda l, j: (l, 0, j)),
            pl.BlockSpec((1, 1, tn), lambda l, j: (l, 0, j)),
        ],
        out_specs=pl.BlockSpec((1, 8, tn), lambda l, j: (l, 0, j)),
        out_shape=jax.ShapeDtypeStruct((DEPTH, 8, N_MOD * D_MODEL), F32),
        compiler_params=pltpu.CompilerParams(
            dimension_semantics=("arbitrary", "arbitrary"), vmem_limit_bytes=VMEM_LIMIT),
        name="modulation",
    )(conds, w_mod, b_mod.reshape(DEPTH, 1, N_MOD * D_MODEL))
    return out.reshape(DEPTH, 8, N_MOD, D_MODEL)


def _ffn_kernel(*refs, sub, final, split_input):
    refs = list(refs)
    x_ref = refs.pop(0)
    xc_ref = refs.pop(0) if split_input else None
    m_ref, g_ref, win_ref, wout_ref = refs[:4]
    fg_ref = refs[4] if final else None
    o_ref, u_ref, h_ref = refs[-3:]
    x = x_ref[...]
    if split_input:
        x = jnp.where(pl.program_id(0) < N_LAT_TILES, x, xc_ref[...])
    m = m_ref[0, 0]
    shift = m[3 * sub:3 * sub + 1]
    scale = m[3 * sub + 1:3 * sub + 2]
    gate = m[3 * sub + 2:3 * sub + 3]
    u_ref[...] = _norm_mod(x, g_ref[...], shift, scale).astype(BF16)
    for c in range(N_FF_CHUNKS):
        lo = c * FF_CHUNK
        a = _dot(u_ref[...], win_ref[0, 0, :, lo:lo + FF_CHUNK])
        b = _dot(u_ref[...], win_ref[0, 0, :, D_FF + lo:D_FF + lo + FF_CHUNK])
        h_ref[:, lo:lo + FF_CHUNK] = (a * _sigmoid(a) * b).astype(BF16)
    y = x + 0.5 * gate * _dot(h_ref[...], wout_ref[0, 0])
    if final:
        var = jnp.mean(y * y, axis=-1, keepdims=True)
        y = y * lax.rsqrt(var + EPS) * fg_ref[...]
    o_ref[...] = y


def _ffn(x, m_l, g, w_in_c, w_out_c, *, wsel, sub, n_tiles, final_g=None, x_ctx=None):
    final = final_g is not None
    split_input = x_ctx is not None
    const = dict(pipeline_mode=pl.Buffered(1))
    if split_input:
        in_specs = [pl.BlockSpec((TM, D_MODEL), lambda i: (jnp.minimum(i, N_LAT_TILES - 1), 0)),
                    pl.BlockSpec((TM, D_MODEL), lambda i: (0, 0))]
        args = [x, x_ctx]
    else:
        in_specs = [pl.BlockSpec((TM, D_MODEL), lambda i: (i, 0))]
        args = [x]
    in_specs += [
        pl.BlockSpec((1, 1, N_MOD, D_MODEL), lambda i: (0, _mod_row(i), 0, 0)),
        pl.BlockSpec((1, D_MODEL), lambda i: (0, 0)),
        pl.BlockSpec((1, 1, D_MODEL, 2 * D_FF), lambda i: wsel + (0, 0), **const),
        pl.BlockSpec((1, 1, D_FF, D_MODEL), lambda i: wsel + (0, 0), **const),
    ]
    args += [m_l, g.reshape(1, D_MODEL), w_in_c, w_out_c]
    if final:
        in_specs.append(pl.BlockSpec((1, D_MODEL), lambda i: (0, 0)))
        args.append(final_g.reshape(1, D_MODEL))
    return pl.pallas_call(
        functools.partial(_ffn_kernel, sub=sub, final=final, split_input=split_input),
        grid=(n_tiles,),
        in_specs=in_specs,
        out_specs=pl.BlockSpec((TM, D_MODEL), lambda i: (i, 0)),
        out_shape=jax.ShapeDtypeStruct((n_tiles * TM, D_MODEL), F32),
        scratch_shapes=[pltpu.VMEM((TM, D_MODEL), BF16), pltpu.VMEM((TM, D_FF), BF16)],
        compiler_params=pltpu.CompilerParams(
            dimension_semantics=("arbitrary",), vmem_limit_bytes=VMEM_LIMIT),
        name="ffn",
    )(*args)


def _proj_kernel(x_ref, m_ref, g_ref, w_ref, cos_ref, sa_ref, sb_ref, wc_ref, bg_ref,
                 q_ref, k_ref, vt_ref, f_ref, ga_ref, gf_ref, u_ref):
    x = x_ref[...]
    m = m_ref[0, 0]
    u_ref[...] = _norm_mod(x, g_ref[...], m[3:4], m[4:5]).astype(BF16)
    cw = 256
    cos = jnp.concatenate([cos_ref[...]] * (cw // 128), axis=1)
    sa = jnp.concatenate([sa_ref[...]] * (cw // 128), axis=1)
    sb = jnp.concatenate([sb_ref[...]] * (cw // 128), axis=1)

    def col(off):
        return _dot(u_ref[...], w_ref[0, :, off:off + cw])

    for out_ref, base, qscale in ((q_ref, 0, HEAD_DIM ** -0.5 * math.log2(math.e)), (k_ref, QK_W, 1.0)):
        for j in range(QK_W // cw):
            p = col(base + j * cw)
            r = p * cos + pltpu.roll(p, cw - 16, 1) * sa + pltpu.roll(p, 16, 1) * sb
            if qscale != 1.0:
                r = r * qscale
            out_ref[:, j * cw:(j + 1) * cw] = r.astype(BF16)
    for j in range(ATTN_WIDTH // cw):
        vt_ref[j * cw:(j + 1) * cw, :] = col(2 * QK_W + j * cw).T.astype(BF16)
    for j in range(FOURIER_WIDTH // cw):
        p = col(F_OFF + j * cw).astype(BF16)
        for t in range(cw // FGROUP_DIM):
            grp = j * (cw // FGROUP_DIM) + t
            r = _dot(p[:, t * FGROUP_DIM:(t + 1) * FGROUP_DIM], wc_ref[...])
            shape3 = (TM // DFT_B, DFT_B, FGROUP_DIM)
            f_ref[:, :, grp * FGROUP_DIM:(grp + 1) * FGROUP_DIM] = r[:, :FGROUP_DIM].astype(BF16).reshape(shape3)
            f_ref[:, :, FOURIER_WIDTH + grp * FGROUP_DIM:FOURIER_WIDTH + (grp + 1) * FGROUP_DIM] = (
                r[:, FGROUP_DIM:].astype(BF16).reshape(shape3))
    for out_ref, base, brow in ((ga_ref, G_OFF, 0), (gf_ref, G_OFF + D_MODEL, 1)):
        for j in range(D_MODEL // cw):
            p = col(base + j * cw) + bg_ref[brow:brow + 1, j * cw:(j + 1) * cw]
            out_ref[:, j * cw:(j + 1) * cw] = _sigmoid(p).astype(BF16)


def _proj(x, m_l, g, w_in_b, tabs, wc, b_gate, *, layer):
    cos, sa, sb = tabs
    const = dict(pipeline_mode=pl.Buffered(1))
    row = lambda i: (i, 0)
    fix = lambda i: (0, 0)
    wide = jax.ShapeDtypeStruct((R_ALL, D_MODEL), BF16)
    row_spec = pl.BlockSpec((TM, D_MODEL), row)
    return pl.pallas_call(
        _proj_kernel,
        grid=(N_ALL_TILES,),
        in_specs=[
            pl.BlockSpec((TM, D_MODEL), row),
            pl.BlockSpec((1, 1, N_MOD, D_MODEL), lambda i: (0, _mod_row(i), 0, 0)),
            pl.BlockSpec((1, D_MODEL), fix),
            pl.BlockSpec((1, D_MODEL, IN_WIDTH), lambda i: (layer, 0, 0), **const),
            pl.BlockSpec((TM, 128), row),
            pl.BlockSpec((TM, 128), row),
            pl.BlockSpec((TM, 128), row),
            pl.BlockSpec((FGROUP_DIM, 2 * FGROUP_DIM), fix),
            pl.BlockSpec((2, D_MODEL), fix),
        ],
        out_specs=[row_spec, row_spec, pl.BlockSpec((ATTN_WIDTH, TM), lambda i: (0, i)),
                   pl.BlockSpec((TM // DFT_B, DFT_B, 2 * FOURIER_WIDTH), lambda i: (i, 0, 0)), row_spec, row_spec],
        out_shape=[wide, wide, jax.ShapeDtypeStruct((ATTN_WIDTH, R_ALL), BF16),
                   jax.ShapeDtypeStruct((R_ALL // DFT_B, DFT_B, 2 * FOURIER_WIDTH), BF16), wide, wide],
        scratch_shapes=[pltpu.VMEM((TM, D_MODEL), BF16)],
        compiler_params=pltpu.CompilerParams(
            dimension_semantics=("arbitrary",), vmem_limit_bytes=VMEM_LIMIT),
        name="proj",
    )(x, m_l, g.reshape(1, D_MODEL), w_in_b, cos, sa, sb, wc, b_gate)


def _masked_queries(q):
    lane = lax.broadcasted_iota(jnp.int32, q.shape, 1)
    zero = jnp.zeros_like(q)
    return jnp.where(lane < HEAD_DIM, q, zero), jnp.where(lane >= HEAD_DIM, q, zero)


def _lambda(lp, lam_init):
    return (jnp.exp(jnp.sum(lp[0:1] * lp[1:2], axis=-1, keepdims=True))
            - jnp.exp(jnp.sum(lp[2:3] * lp[3:4], axis=-1, keepdims=True)) + lam_init)


def _attn_epilogue(acc, l, lam, sg, lam_init):
    n = acc.shape[1] // 2
    o = acc / l
    d = o[:, :n] - lam * o[:, n:]
    var = jnp.mean(d * d, axis=0, keepdims=True)
    y = d * lax.rsqrt(var + EPS) * sg * (1.0 - lam_init)
    return y.T.astype(BF16)


def _attn_kernel(lp_ref, sg_ref, q_ref, kc_ref, vc_ref, kl_ref, vl_ref, o_ref,
                 qs_ref, pacc_ref, pl_ref, qx_ref, m_ref, l_ref, acc_ref, *, lam_init):
    lam = _lambda(lp_ref[...], lam_init)

    def tile_rows(t):
        if isinstance(t, int):
            return pl.ds(t * TQ, TQ)
        return pl.ds(pl.multiple_of(t * TQ, TQ), TQ)

    def stage_queries(dst, t):
        q0, q1 = _masked_queries(q_ref[tile_rows(t), :])
        dst[0:TQ, :] = q0
        dst[TQ:, :] = q1

    def scores(keys, queries_ref):
        return lax.dot_general(keys, queries_ref[...], (((1,), (1,)), ((), ())), preferred_element_type=F32)

    def fold8(fn, a):
        return fn(a.reshape(a.shape[0] // 8, 8, 2 * TQ), axis=0)

    def latent_block(blk):
        return kl_ref[blk * TK:(blk + 1) * TK, :], vl_ref[:, blk * TK:(blk + 1) * TK]

    def store_tile(t, acc, l8):
        l = jnp.sum(l8, axis=0, keepdims=True)
        o_ref[tile_rows(t), :] = _attn_epilogue(acc, l, lam, sg_ref[...], lam_init)

    def exact_tile(t):
        stage_queries(qx_ref, t)
        m_ref[...] = jnp.full_like(m_ref, -jnp.inf)
        l_ref[...] = jnp.zeros_like(l_ref)
        acc_ref[...] = jnp.zeros_like(acc_ref)

        def update(keys, vt):
            s = scores(keys, qx_ref)
            m_old = m_ref[...]
            m_new = jnp.maximum(m_old, jnp.max(s, axis=0, keepdims=True))
            alpha = jnp.exp2(m_old - m_new)
            p = jnp.exp2(s - m_new)
            m_ref[...] = m_new
            l_ref[...] = alpha * l_ref[...] + fold8(jnp.sum, p)
            acc_ref[...] = alpha * acc_ref[...] + _dot(vt, p.astype(BF16))

        def latent(blk, carry):
            off = pl.multiple_of(blk * TK, TK)
            update(kl_ref[pl.ds(off, TK), :], vl_ref[:, pl.ds(off, TK)])
            return carry

        update(kc_ref[...], vc_ref[...])
        lax.fori_loop(0, N_LAT_BLOCKS, latent, 0)
        store_tile(t, acc_ref[...], l_ref[...])

    def tile(t, redo_prev):
        stage_queries(qs_ref, t)
        s = scores(kc_ref[...], qs_ref)
        m0 = jnp.max(s, axis=0, keepdims=True)
        p = jnp.exp2(s - m0)
        l8 = fold8(jnp.sum, p)
        acc = _dot(vc_ref[...], p.astype(BF16))
        for blk in range(N_LAT_BLOCKS):
            keys, vt = latent_block(blk)
            p = jnp.exp2(scores(keys, qs_ref) - m0)
            l8 = l8 + fold8(jnp.sum, p)
            acc = acc + _dot(vt, p.astype(BF16))
        prev = jnp.maximum(t - 1, 0)
        store_tile(prev, pacc_ref[...], pl_ref[...])
        pacc_ref[...] = acc
        pl_ref[...] = l8
        redo = (jnp.max(l8) > PROB_SUM_LIMIT).astype(jnp.int32)

        @pl.when(redo_prev == 1)
        def _():
            exact_tile(prev)

        return redo

    pacc_ref[...] = jnp.ones_like(pacc_ref)
    pl_ref[...] = jnp.ones_like(pl_ref)
    redo_last = lax.fori_loop(0, N_LAT_QT, tile, jnp.int32(0))
    store_tile(N_LAT_QT - 1, pacc_ref[...], pl_ref[...])

    @pl.when(redo_last == 1)
    def _():
        exact_tile(N_LAT_QT - 1)


def _attn_ctx_kernel(lp_ref, sg_ref, q_ref, k_ref, vt_ref, o_ref, *, lam_init):
    q0, q1 = _masked_queries(q_ref[...])
    qs = jnp.concatenate([q0, q1], axis=0)
    s = lax.dot_general(k_ref[...], qs, (((1,), (1,)), ((), ())), preferred_element_type=F32)
    p = jnp.exp2(s - jnp.max(s, axis=0, keepdims=True))
    l = jnp.sum(p, axis=0, keepdims=True)
    acc = _dot(vt_ref[...], p.astype(BF16))
    o_ref[...] = _attn_epilogue(acc, l, _lambda(lp_ref[...], lam_init), sg_ref[...], lam_init)


def _attention(q, k, vt, lam_params_l, subln_g_l, *, lam_init, with_ctx_queries):
    ctx_blk0 = R_LAT // CTX_LEN
    sg = subln_g_l.reshape(V_DIM, 1)
    attn = pl.pallas_call(
        functools.partial(_attn_kernel, lam_init=lam_init),
        grid=(BATCH, N_HEADS),
        in_specs=[
            pl.BlockSpec((4, HEAD_DIM), lambda b, h: (0, 0)),
            pl.BlockSpec((V_DIM, 1), lambda b, h: (0, 0)),
            pl.BlockSpec((SEQ, V_DIM), lambda b, h: (b, h)),
            pl.BlockSpec((CTX_LEN, V_DIM), lambda b, h: (ctx_blk0 + b, h)),
            pl.BlockSpec((V_DIM, CTX_LEN), lambda b, h: (h, ctx_blk0 + b)),
            pl.BlockSpec((SEQ, V_DIM), lambda b, h: (b, h)),
            pl.BlockSpec((V_DIM, SEQ), lambda b, h: (h, b)),
        ],
        out_specs=pl.BlockSpec((SEQ, V_DIM), lambda b, h: (b, h)),
        out_shape=jax.ShapeDtypeStruct((R_LAT, ATTN_WIDTH), BF16),
        scratch_shapes=[
            pltpu.VMEM((2 * TQ, V_DIM), BF16),
            pltpu.VMEM((V_DIM, 2 * TQ), F32),
            pltpu.VMEM((8, 2 * TQ), F32),
            pltpu.VMEM((2 * TQ, V_DIM), BF16),
            pltpu.VMEM((1, 2 * TQ), F32),
            pltpu.VMEM((8, 2 * TQ), F32),
            pltpu.VMEM((V_DIM, 2 * TQ), F32),
        ],
        compiler_params=pltpu.CompilerParams(
            dimension_semantics=("arbitrary", "arbitrary"), vmem_limit_bytes=VMEM_LIMIT),
        name="diff_attn",
    )(lam_params_l, sg, q, k, vt, k, vt)
    if not with_ctx_queries:
        return attn, None
    attn_ctx = pl.pallas_call(
        functools.partial(_attn_ctx_kernel, lam_init=lam_init),
        grid=(BATCH, N_HEADS),
        in_specs=[
            pl.BlockSpec((4, HEAD_DIM), lambda b, h: (0, 0)),
            pl.BlockSpec((V_DIM, 1), lambda b, h: (0, 0)),
            pl.BlockSpec((CTX_LEN, V_DIM), lambda b, h: (ctx_blk0 + b, h)),
            pl.BlockSpec((CTX_LEN, V_DIM), lambda b, h: (ctx_blk0 + b, h)),
            pl.BlockSpec((V_DIM, CTX_LEN), lambda b, h: (h, ctx_blk0 + b)),
        ],
        out_specs=pl.BlockSpec((CTX_LEN, V_DIM), lambda b, h: (b, h)),
        out_shape=jax.ShapeDtypeStruct((BATCH * CTX_LEN, ATTN_WIDTH), BF16),
        compiler_params=pltpu.CompilerParams(
            dimension_semantics=("arbitrary", "arbitrary"), vmem_limit_bytes=VMEM_LIMIT),
        name="diff_attn_ctx",
    )(lam_params_l, sg, q, k, vt)
    return attn, attn_ctx


def _dft1_kernel(x_ref, w_ref, twr_ref, twi_ref, o_ref):
    n = DFT_A * NS2
    x = x_ref[...].reshape(n, 2 * FOURIER_WIDTH)
    xs = jnp.concatenate([x[:, :FOURIER_WIDTH], x[:, FOURIER_WIDTH:]], axis=0)
    r = _dot(w_ref[...], xs)
    re, im = r[:n], r[n:]
    twr, twi = twr_ref[0], twi_ref[0]
    o_ref[:, :, :FOURIER_WIDTH] = (re * twr - im * twi).astype(BF16).reshape(DFT_A, NS2, FOURIER_WIDTH)
    o_ref[:, :, FOURIER_WIDTH:] = (re * twi + im * twr).astype(BF16).reshape(DFT_A, NS2, FOURIER_WIDTH)


def _dft2_kernel(x_ref, w_ref, o_ref, *, n, nk, scale):
    g = n // DFT_B
    for j in range(nk):
        x = x_ref[j * g:(j + 1) * g].reshape(n, 2 * FOURIER_WIDTH)
        xs = jnp.concatenate([x[:, :FOURIER_WIDTH], x[:, FOURIER_WIDTH:]], axis=0)
        r = _dot(w_ref[...], xs) * scale
        o_ref[:, j * FOURIER_WIDTH:(j + 1) * FOURIER_WIDTH] = r.astype(BF16)


def _fourier_latent(f_all, t1, w2):
    w1, twr, twi = t1
    blk = (DFT_A, NS2, 2 * FOURIER_WIDTH)
    n = DFT_A * NS2
    x2 = pl.pallas_call(
        _dft1_kernel,
        grid=(BATCH, DFT_B // NS2),
        in_specs=[
            pl.BlockSpec(blk, lambda b, j: (b, j, 0)),
            pl.BlockSpec((2 * n, 2 * n), lambda b, j: (0, 0)),
            pl.BlockSpec((1, n, 1), lambda b, j: (j, 0, 0)),
            pl.BlockSpec((1, n, 1), lambda b, j: (j, 0, 0)),
        ],
        out_specs=pl.BlockSpec(blk, lambda b, j: (b, j, 0)),
        out_shape=jax.ShapeDtypeStruct((BATCH * DFT_A, DFT_B, 2 * FOURIER_WIDTH), BF16),
        compiler_params=pltpu.CompilerParams(
            dimension_semantics=("arbitrary", "arbitrary"), vmem_limit_bytes=VMEM_LIMIT),
        name="dft_stage1",
    )(f_all, w1, twr, twi)
    y = pl.pallas_call(
        functools.partial(_dft2_kernel, n=DFT_B, nk=NK1, scale=(SEQ * FGROUP_DIM) ** -0.5),
        grid=(BATCH, DFT_A // NK1),
        in_specs=[
            pl.BlockSpec((NK1, DFT_B, 2 * FOURIER_WIDTH), lambda b, j: (b * (DFT_A // NK1) + j, 0, 0)),
            pl.BlockSpec((DFT_B, 2 * DFT_B), lambda b, j: (0, 0)),
        ],
        out_specs=pl.BlockSpec((DFT_B, NK1 * FOURIER_WIDTH), lambda b, j: (b, j)),
        out_shape=jax.ShapeDtypeStruct((R_LAT // DFT_A, DFT_A * FOURIER_WIDTH), BF16),
        compiler_params=pltpu.CompilerParams(
            dimension_semantics=("arbitrary", "arbitrary"), vmem_limit_bytes=VMEM_LIMIT),
        name="dft_stage2",
    )(x2, w2)
    return y.reshape(R_LAT, FOURIER_WIDTH)


def _fourier_ctx(f_all, wctx):
    ctx_blk0 = R_LAT // CTX_LEN
    return pl.pallas_call(
        functools.partial(_dft2_kernel, n=CTX_LEN, nk=1, scale=(CTX_LEN * FGROUP_DIM) ** -0.5),
        grid=(BATCH,),
        in_specs=[
            pl.BlockSpec((CTX_LEN // DFT_B, DFT_B, 2 * FOURIER_WIDTH), lambda b: (ctx_blk0 + b, 0, 0)),
            pl.BlockSpec((CTX_LEN, 2 * CTX_LEN), lambda b: (0, 0)),
        ],
        out_specs=pl.BlockSpec((CTX_LEN, FOURIER_WIDTH), lambda b: (b, 0)),
        out_shape=jax.ShapeDtypeStruct((BATCH * CTX_LEN, FOURIER_WIDTH), BF16),
        compiler_params=pltpu.CompilerParams(
            dimension_semantics=("arbitrary",), vmem_limit_bytes=VMEM_LIMIT),
        name="dft_ctx",
    )(f_all, wctx)


def _merge_kernel(*refs, split_input):
    refs = list(refs)
    x_ref, m_ref, a_ref, f_ref = refs[:4]
    ac_ref, fc_ref = (refs[4], refs[5]) if split_input else (None, None)
    ga_ref, gf_ref, wa_ref, wf_ref, wo_ref, o_ref = refs[-6:]
    attn, four = a_ref[...], f_ref[...]
    if split_input:
        latent = pl.program_id(0) < N_LAT_TILES
        attn = jnp.where(latent, attn, ac_ref[...])
        four = jnp.where(latent, four, fc_ref[...])
    a = _dot(attn, wa_ref[0])
    fr = _dot(four, wf_ref[0])
    merged = ga_ref[...].astype(F32) * a + gf_ref[...].astype(F32) * fr
    mix = _dot(merged.astype(BF16), wo_ref[0])
    o_ref[...] = x_ref[...] + m_ref[0, 0][5:6] * mix


def _merge(x, m_l, attn, four, ga, gf, wa, wf, wo, *, layer, n_tiles, attn_ctx=None, four_ctx=None):
    split_input = attn_ctx is not None
    const = dict(pipeline_mode=pl.Buffered(1))
    row = lambda i: (i, 0)
    lat_row = (lambda i: (jnp.minimum(i, N_LAT_TILES - 1), 0)) if split_input else row
    fix = lambda i: (0, 0)
    wsel = lambda i: (layer, 0, 0)
    in_specs = [
        pl.BlockSpec((TM, D_MODEL), row),
        pl.BlockSpec((1, 1, N_MOD, D_MODEL), lambda i: (0, _mod_row(i), 0, 0)),
        pl.BlockSpec((TM, ATTN_WIDTH), lat_row),
        pl.BlockSpec((TM, FOURIER_WIDTH), lat_row),
    ]
    args = [x, m_l, attn, four]
    if split_input:
        in_specs += [pl.BlockSpec((TM, ATTN_WIDTH), fix), pl.BlockSpec((TM, FOURIER_WIDTH), fix)]
        args += [attn_ctx, four_ctx]
    in_specs += [
        pl.BlockSpec((TM, D_MODEL), row),
        pl.BlockSpec((TM, D_MODEL), row),
        pl.BlockSpec((1, ATTN_WIDTH, D_MODEL), wsel, **const),
        pl.BlockSpec((1, FOURIER_WIDTH, D_MODEL), wsel, **const),
        pl.BlockSpec((1, D_MODEL, D_MODEL), wsel, **const),
    ]
    args += [ga, gf, wa, wf, wo]
    return pl.pallas_call(
        functools.partial(_merge_kernel, split_input=split_input),
        grid=(n_tiles,),
        in_specs=in_specs,
        out_specs=pl.BlockSpec((TM, D_MODEL), row),
        out_shape=jax.ShapeDtypeStruct((n_tiles * TM, D_MODEL), F32),
        compiler_params=pltpu.CompilerParams(
            dimension_semantics=("arbitrary",), vmem_limit_bytes=VMEM_LIMIT),
        name="merge",
    )(*args)


def _rope_tables():
    s = jnp.arange(SEQ, dtype=jnp.int32)
    lane = jnp.arange(128, dtype=jnp.int32) % HEAD_DIM
    inv = 1.0 / (ROPE_BASE ** ((2 * (lane % (AXIS_DIM // 2))).astype(F32) / AXIS_DIM))
    pos = jnp.where(lane[None, :] < AXIS_DIM, (s // GRID_W)[:, None], (s % GRID_W)[:, None]).astype(F32)
    ang = pos * inv[None, :]
    cos, sin = jnp.cos(ang), jnp.sin(ang)
    first = (lane % AXIS_DIM) < (AXIS_DIM // 2)
    sa = jnp.where(first[None, :], -sin, 0.0)
    sb = jnp.where(first[None, :], 0.0, sin)
    n_ctx = BATCH * CTX_LEN

    def full(t, fill):
        return jnp.concatenate([t] * BATCH + [jnp.full((n_ctx, 128), fill, F32)], axis=0)

    return full(cos, 1.0), full(sa, 0.0), full(sb, 0.0)


def _angles(num, den):
    return (2.0 * math.pi / den) * (num % den).astype(F32)


def _dft_tables():
    c = jnp.arange(FGROUP_DIM, dtype=jnp.int32)
    ang = _angles(c[:, None] * c[None, :], FGROUP_DIM)
    wc = jnp.concatenate([jnp.cos(ang), -jnp.sin(ang)], axis=1).astype(BF16)

    n = DFT_A * NS2
    a = jnp.arange(DFT_A, dtype=jnp.int32)
    ang = _angles(a[:, None] * a[None, :], DFT_A)
    cs, sn = jnp.cos(ang), jnp.sin(ang)
    m1 = jnp.concatenate([jnp.concatenate([cs, sn], axis=1),
                          jnp.concatenate([-sn, cs], axis=1)], axis=0)
    idx = jnp.arange(2 * n, dtype=jnp.int32)
    spread = (idx[:, None] // NS2 == jnp.arange(2 * DFT_A, dtype=jnp.int32)[None, :]).astype(F32)
    same_pos = idx[:, None] % NS2 == idx[None, :] % NS2
    w1 = jnp.where(same_pos, spread @ m1 @ spread.T, 0.0).astype(BF16)
    s2 = jnp.arange(DFT_B, dtype=jnp.int32).reshape(DFT_B // NS2, 1, NS2)
    ang = _angles(a[None, :, None] * s2, SEQ).reshape(DFT_B // NS2, n, 1)
    t1 = (w1, jnp.cos(ang), -jnp.sin(ang))

    b = jnp.arange(DFT_B, dtype=jnp.int32)
    ang = _angles(b[:, None] * b[None, :], DFT_B)
    w2 = jnp.concatenate([jnp.cos(ang), jnp.sin(ang)], axis=1).astype(BF16)

    n = jnp.arange(CTX_LEN, dtype=jnp.int32)
    ang = _angles(n[:, None] * n[None, :], CTX_LEN)
    wctx = jnp.concatenate([jnp.cos(ang), jnp.sin(ang)], axis=1).astype(BF16)
    return wc, t1, w2, wctx


def kernel(x, c, ctx, c_ctx, w_mod, b_mod, norm_g, ffn_w_in, ffn_w_out, w_in, b_gate,
           lam_params, subln_g, w_attn_out, w_four_out, w_o, final_g):
    xs = x.reshape(R_LAT, D_MODEL)
    x_ctx = ctx.reshape(BATCH * CTX_LEN, D_MODEL)
    conds = jnp.concatenate([c, c_ctx[None, :], jnp.zeros((8 - BATCH - 1, D_MODEL), F32)], axis=0)
    m_all = _modulation(conds, w_mod, b_mod)
    tabs = _rope_tables()
    wc, t1, w2, wctx = _dft_tables()

    w_ffn_in, w_ffn_out = ffn_w_in.astype(BF16), ffn_w_out.astype(BF16)
    w_in_b = w_in.astype(BF16)
    wa, wf, wo = w_attn_out.astype(BF16), w_four_out.astype(BF16), w_o.astype(BF16)

    for l in range(DEPTH):
        last = l == DEPTH - 1
        lam_init = 0.8 - 0.6 * math.exp(-0.3 * l)
        m_l = m_all[l:l + 1]
        xs = _ffn(xs, m_l, norm_g[l, 0], w_ffn_in, w_ffn_out, wsel=(l, 0), sub=0, n_tiles=N_ALL_TILES,
                  x_ctx=x_ctx if l == 0 else None)
        q, k, vt, f, ga, gf = _proj(xs, m_l, norm_g[l, 1], w_in_b, tabs, wc, b_gate[l], layer=l)
        attn, attn_ctx = _attention(q, k, vt, lam_params[l], subln_g[l], lam_init=lam_init,
                                    with_ctx_queries=not last)
        four = _fourier_latent(f, t1, w2)
        four_ctx = None if last else _fourier_ctx(f, wctx)
        n_tiles = N_LAT_TILES if last else N_ALL_TILES
        xs = _merge(xs, m_l, attn, four, ga, gf, wa, wf, wo, layer=l, n_tiles=n_tiles,
                    attn_ctx=attn_ctx, four_ctx=four_ctx)
        xs = _ffn(xs, m_l, norm_g[l, 2], w_ffn_in, w_ffn_out, wsel=(l, 1), sub=2, n_tiles=n_tiles,
                  final_g=final_g if last else None)
    return xs.reshape(BATCH, SEQ, D_MODEL)
```

```python
import functools
import math

import jax
import jax.numpy as jnp
from jax import lax
from jax.experimental import pallas as pl
from jax.experimental.pallas import tpu as pltpu

D_MODEL = 1024
BATCH = 2
SEQ = 8192
DEPTH = 2
CTX_LEN = 256
GRID_W = 64
N_HEADS = 8
HEAD_DIM = 64
V_DIM = 2 * HEAD_DIM
QK_W = N_HEADS * 2 * HEAD_DIM
ATTN_WIDTH = N_HEADS * V_DIM
N_FGROUPS = 4
FGROUP_DIM = 128
FOURIER_WIDTH = N_FGROUPS * FGROUP_DIM
F_OFF = 2 * QK_W + ATTN_WIDTH
G_OFF = F_OFF + FOURIER_WIDTH
IN_WIDTH = G_OFF + 2 * D_MODEL
D_FF = 2816
AXIS_DIM = HEAD_DIM // 2
ROPE_BASE = 10000.0
N_MOD = 9
EPS = 1e-6

R_LAT = BATCH * SEQ
R_ALL = R_LAT + BATCH * CTX_LEN
TM = 512
N_LAT_TILES = R_LAT // TM
N_ALL_TILES = R_ALL // TM
FF_CHUNK = 256
N_FF_CHUNKS = D_FF // FF_CHUNK
TQ = 256
TK = 1024
N_LAT_BLOCKS = SEQ // TK
PROB_SUM_LIMIT = 2.0 ** 24
N_LAT_QT = SEQ // TQ
DFT_A = 64
DFT_B = 128
NS2 = 16
NK1 = 8
VMEM_LIMIT = 56 * 1024 * 1024

F32 = jnp.float32
BF16 = jnp.bfloat16


def _dot(a, b):
    return jnp.dot(a, b, preferred_element_type=F32)


def _sigmoid(x):
    return 1.0 / (1.0 + jnp.exp(-x))


def _mod_row(i):
    return jnp.minimum(i // (SEQ // TM), 2)


def _norm_mod(x, g, shift, scale):
    var = jnp.mean(x * x, axis=-1, keepdims=True)
    return x * lax.rsqrt(var + EPS) * (g * (1.0 + scale)) + shift


def _mod_kernel(c_ref, w_ref, b_ref, o_ref):
    c = c_ref[...]
    s = c * _sigmoid(c)
    o_ref[0] = jnp.dot(s, w_ref[0], preferred_element_type=F32,
                       precision=lax.Precision.HIGHEST) + b_ref[0]


def _modulation(conds, w_mod, b_mod):
    tn = D_MODEL
    out = pl.pallas_call(
        _mod_kernel,
        grid=(DEPTH, N_MOD),
        in_specs=[
            pl.BlockSpec((8, D_MODEL), lambda l, j: (0, 0)),
            pl.BlockSpec((1, D_MODEL, tn), lambda l, j: (l, 0, j)),
            pl.BlockSpec((1, 1, tn), lambda l, j: (l, 0, j)),
        ],
        out_specs=pl.BlockSpec((1, 8, tn), lambda l, j: (l, 0, j)),
        out_shape=jax.ShapeDtypeStruct((DEPTH, 8, N_MOD * D_MODEL), F32),
        compiler_params=pltpu.CompilerParams(
            dimension_semantics=("arbitrary", "arbitrary"), vmem_limit_bytes=VMEM_LIMIT),
        name="modulation",
    )(conds, w_mod, b_mod.reshape(DEPTH, 1, N_MOD * D_MODEL))
    return out.reshape(DEPTH, 8, N_MOD, D_MODEL)


def _ffn_kernel(*refs, sub, final, split_input):
    refs = list(refs)
    x_ref = refs.pop(0)
    xc_ref = refs.pop(0) if split_input else None
    m_ref, g_ref, win_ref, wout_ref = refs[:4]
    fg_ref = refs[4] if final else None
    o_ref, u_ref, h_ref = refs[-3:]
    x = x_ref[...]
    if split_input:
        x = jnp.where(pl.program_id(0) < N_LAT_TILES, x, xc_ref[...])
    m = m_ref[0, 0]
    shift = m[3 * sub:3 * sub + 1]
    scale = m[3 * sub + 1:3 * sub + 2]
    gate = m[3 * sub + 2:3 * sub + 3]
    u_ref[...] = _norm_mod(x, g_ref[...], shift, scale).astype(BF16)
    for c in range(N_FF_CHUNKS):
        lo = c * FF_CHUNK
        a = _dot(u_ref[...], win_ref[0, 0, :, lo:lo + FF_CHUNK])
        b = _dot(u_ref[...], win_ref[0, 0, :, D_FF + lo:D_FF + lo + FF_CHUNK])
        h_ref[:, lo:lo + FF_CHUNK] = (a * _sigmoid(a) * b).astype(BF16)
    y = x + 0.5 * gate * _dot(h_ref[...], wout_ref[0, 0])
    if final:
        var = jnp.mean(y * y, axis=-1, keepdims=True)
        y = y * lax.rsqrt(var + EPS) * fg_ref[...]
    o_ref[...] = y


def _ffn(x, m_l, g, w_in_c, w_out_c, *, wsel, sub, n_tiles, final_g=None, x_ctx=None):
    final = final_g is not None
    split_input = x_ctx is not None
    const = dict(pipeline_mode=pl.Buffered(1))
    if split_input:
        in_specs = [pl.BlockSpec((TM, D_MODEL), lambda i: (jnp.minimum(i, N_LAT_TILES - 1), 0)),
                    pl.BlockSpec((TM, D_MODEL), lambda i: (0, 0))]
        args = [x, x_ctx]
    else:
        in_specs = [pl.BlockSpec((TM, D_MODEL), lambda i: (i, 0))]
        args = [x]
    in_specs += [
        pl.BlockSpec((1, 1, N_MOD, D_MODEL), lambda i: (0, _mod_row(i), 0, 0)),
        pl.BlockSpec((1, D_MODEL), lambda i: (0, 0)),
        pl.BlockSpec((1, 1, D_MODEL, 2 * D_FF), lambda i: wsel + (0, 0), **const),
        pl.BlockSpec((1, 1, D_FF, D_MODEL), lambda i: wsel + (0, 0), **const),
    ]
    args += [m_l, g.reshape(1, D_MODEL), w_in_c, w_out_c]
    if final:
        in_specs.append(pl.BlockSpec((1, D_MODEL), lambda i: (0, 0)))
        args.append(final_g.reshape(1, D_MODEL))
    return pl.pallas_call(
        functools.partial(_ffn_kernel, sub=sub, final=final, split_input=split_input),
        grid=(n_tiles,),
        in_specs=in_specs,
        out_specs=pl.BlockSpec((TM, D_MODEL), lambda i: (i, 0)),
        out_shape=jax.ShapeDtypeStruct((n_tiles * TM, D_MODEL), F32),
        scratch_shapes=[pltpu.VMEM((TM, D_MODEL), BF16), pltpu.VMEM((TM, D_FF), BF16)],
        compiler_params=pltpu.CompilerParams(
            dimension_semantics=("arbitrary",), vmem_limit_bytes=VMEM_LIMIT),
        name="ffn",
    )(*args)


def _proj_kernel(x_ref, m_ref, g_ref, w_ref, rrow_ref, rcol_ref, wc_ref, bg_ref,
                 q_ref, k_ref, vt_ref, f_ref, ga_ref, gf_ref, u_ref):
    x = x_ref[...]
    m = m_ref[0, 0]
    u_ref[...] = _norm_mod(x, g_ref[...], m[3:4], m[4:5]).astype(BF16)
    cw = 256

    def rope_table(t):
        per_row = rrow_ref[t]
        rows = jnp.broadcast_to(per_row[:, None, :], (TM // GRID_W, GRID_W, 128)).reshape(TM, 128)
        tab = rows + rcol_ref[0, t]
        return jnp.concatenate([tab] * (cw // 128), axis=1)

    cos, sa, sb = rope_table(0), rope_table(1), rope_table(2)

    def col(off):
        return _dot(u_ref[...], w_ref[0, :, off:off + cw])

    for out_ref, base, qscale in ((q_ref, 0, HEAD_DIM ** -0.5 * math.log2(math.e)), (k_ref, QK_W, 1.0)):
        for j in range(QK_W // cw):
            p = col(base + j * cw)
            r = p * cos + pltpu.roll(p, cw - 16, 1) * sa + pltpu.roll(p, 16, 1) * sb
            if qscale != 1.0:
                r = r * qscale
            out_ref[:, j * cw:(j + 1) * cw] = r.astype(BF16)
    for j in range(ATTN_WIDTH // cw):
        vt_ref[j * cw:(j + 1) * cw, :] = col(2 * QK_W + j * cw).T.astype(BF16)
    for j in range(FOURIER_WIDTH // cw):
        p = col(F_OFF + j * cw).astype(BF16)
        for t in range(cw // FGROUP_DIM):
            grp = j * (cw // FGROUP_DIM) + t
            r = _dot(p[:, t * FGROUP_DIM:(t + 1) * FGROUP_DIM], wc_ref[...])
            shape3 = (TM // DFT_B, DFT_B, FGROUP_DIM)
            f_ref[:, :, grp * FGROUP_DIM:(grp + 1) * FGROUP_DIM] = r[:, :FGROUP_DIM].astype(BF16).reshape(shape3)
            f_ref[:, :, FOURIER_WIDTH + grp * FGROUP_DIM:FOURIER_WIDTH + (grp + 1) * FGROUP_DIM] = (
                r[:, FGROUP_DIM:].astype(BF16).reshape(shape3))
    for out_ref, base, brow in ((ga_ref, G_OFF, 0), (gf_ref, G_OFF + D_MODEL, 1)):
        for j in range(D_MODEL // cw):
            p = col(base + j * cw) + bg_ref[brow:brow + 1, j * cw:(j + 1) * cw]
            out_ref[:, j * cw:(j + 1) * cw] = _sigmoid(p).astype(BF16)


def _proj(x, m_l, g, w_in_b, tabs, wc, b_gate, *, layer):
    rope_rows, rope_cols = tabs
    const = dict(pipeline_mode=pl.Buffered(1))
    row = lambda i: (i, 0)
    fix = lambda i: (0, 0)
    wide = jax.ShapeDtypeStruct((R_ALL, D_MODEL), BF16)
    row_spec = pl.BlockSpec((TM, D_MODEL), row)
    return pl.pallas_call(
        _proj_kernel,
        grid=(N_ALL_TILES,),
        in_specs=[
            pl.BlockSpec((TM, D_MODEL), row),
            pl.BlockSpec((1, 1, N_MOD, D_MODEL), lambda i: (0, _mod_row(i), 0, 0)),
            pl.BlockSpec((1, D_MODEL), fix),
            pl.BlockSpec((1, D_MODEL, IN_WIDTH), lambda i: (layer, 0, 0), **const),
            pl.BlockSpec((3, TM // GRID_W, 128), lambda i: (0, i, 0)),
            pl.BlockSpec((1, 3, TM, 128), lambda i: (i // N_LAT_TILES, 0, 0, 0)),
            pl.BlockSpec((FGROUP_DIM, 2 * FGROUP_DIM), fix),
            pl.BlockSpec((2, D_MODEL), fix),
        ],
        out_specs=[row_spec, row_spec, pl.BlockSpec((ATTN_WIDTH, TM), lambda i: (0, i)),
                   pl.BlockSpec((TM // DFT_B, DFT_B, 2 * FOURIER_WIDTH), lambda i: (i, 0, 0)), row_spec, row_spec],
        out_shape=[wide, wide, jax.ShapeDtypeStruct((ATTN_WIDTH, R_ALL), BF16),
                   jax.ShapeDtypeStruct((R_ALL // DFT_B, DFT_B, 2 * FOURIER_WIDTH), BF16), wide, wide],
        scratch_shapes=[pltpu.VMEM((TM, D_MODEL), BF16)],
        compiler_params=pltpu.CompilerParams(
            dimension_semantics=("arbitrary",), vmem_limit_bytes=VMEM_LIMIT),
        name="proj",
    )(x, m_l, g.reshape(1, D_MODEL), w_in_b, rope_rows, rope_cols, wc, b_gate)


def _masked_queries(q):
    lane = lax.broadcasted_iota(jnp.int32, q.shape, 1)
    zero = jnp.zeros_like(q)
    return jnp.where(lane < HEAD_DIM, q, zero), jnp.where(lane >= HEAD_DIM, q, zero)


def _lambda(lp, lam_init):
    return (jnp.exp(jnp.sum(lp[0:1] * lp[1:2], axis=-1, keepdims=True))
            - jnp.exp(jnp.sum(lp[2:3] * lp[3:4], axis=-1, keepdims=True)) + lam_init)


def _attn_epilogue(acc, l, lam, sg, lam_init):
    n = acc.shape[1] // 2
    o = acc / l
    d = o[:, :n] - lam * o[:, n:]
    var = jnp.mean(d * d, axis=0, keepdims=True)
    y = d * lax.rsqrt(var + EPS) * sg * (1.0 - lam_init)
    return y.T.astype(BF16)


def _attn_kernel(lp_ref, sg_ref, q_ref, kc_ref, vc_ref, kl_ref, vl_ref, o_ref,
                 qs_ref, pacc_ref, pl_ref, qx_ref, m_ref, l_ref, acc_ref, *, lam_init):
    lam = _lambda(lp_ref[...], lam_init)

    def tile_rows(t):
        if isinstance(t, int):
            return pl.ds(t * TQ, TQ)
        return pl.ds(pl.multiple_of(t * TQ, TQ), TQ)

    def stage_queries(dst, t):
        q0, q1 = _masked_queries(q_ref[tile_rows(t), :])
        dst[0:TQ, :] = q0
        dst[TQ:, :] = q1

    def scores(keys, queries_ref):
        return lax.dot_general(keys, queries_ref[...], (((1,), (1,)), ((), ())), preferred_element_type=F32)

    def fold8(fn, a):
        return fn(a.reshape(a.shape[0] // 8, 8, 2 * TQ), axis=0)

    def latent_block(blk):
        return kl_ref[blk * TK:(blk + 1) * TK, :], vl_ref[:, blk * TK:(blk + 1) * TK]

    def store_tile(t, acc, l8):
        l = jnp.sum(l8, axis=0, keepdims=True)
        o_ref[tile_rows(t), :] = _attn_epilogue(acc, l, lam, sg_ref[...], lam_init)

    def exact_tile(t):
        stage_queries(qx_ref, t)
        m_ref[...] = jnp.full_like(m_ref, -jnp.inf)
        l_ref[...] = jnp.zeros_like(l_ref)
        acc_ref[...] = jnp.zeros_like(acc_ref)

        def update(keys, vt):
            s = scores(keys, qx_ref)
            m_old = m_ref[...]
            m_new = jnp.maximum(m_old, jnp.max(s, axis=0, keepdims=True))
            alpha = jnp.exp2(m_old - m_new)
            p = jnp.exp2(s - m_new)
            m_ref[...] = m_new
            l_ref[...] = alpha * l_ref[...] + fold8(jnp.sum, p)
            acc_ref[...] = alpha * acc_ref[...] + _dot(vt, p.astype(BF16))

        def latent(blk, carry):
            off = pl.multiple_of(blk * TK, TK)
            update(kl_ref[pl.ds(off, TK), :], vl_ref[:, pl.ds(off, TK)])
            return carry

        update(kc_ref[...], vc_ref[...])
        lax.fori_loop(0, N_LAT_BLOCKS, latent, 0)
        store_tile(t, acc_ref[...], l_ref[...])

    def tile(t, redo_prev):
        stage_queries(qs_ref, t)
        s = scores(kc_ref[...], qs_ref)
        m0 = jnp.max(s, axis=0, keepdims=True)
        p = jnp.exp2(s - m0)
        l8 = fold8(jnp.sum, p)
        acc = _dot(vc_ref[...], p.astype(BF16))
        for blk in range(N_LAT_BLOCKS):
            keys, vt = latent_block(blk)
            p = jnp.exp2(scores(keys, qs_ref) - m0)
            l8 = l8 + fold8(jnp.sum, p)
            acc = acc + _dot(vt, p.astype(BF16))
        prev = jnp.maximum(t - 1, 0)
        store_tile(prev, pacc_ref[...], pl_ref[...])
        pacc_ref[...] = acc
        pl_ref[...] = l8
        redo = (jnp.max(l8) > PROB_SUM_LIMIT).astype(jnp.int32)

        @pl.when(redo_prev == 1)
        def _():
            exact_tile(prev)

        return redo

    pacc_ref[...] = jnp.ones_like(pacc_ref)
    pl_ref[...] = jnp.ones_like(pl_ref)
    redo_last = lax.fori_loop(0, N_LAT_QT, tile, jnp.int32(0))
    store_tile(N_LAT_QT - 1, pacc_ref[...], pl_ref[...])

    @pl.when(redo_last == 1)
    def _():
        exact_tile(N_LAT_QT - 1)


def _attn_ctx_kernel(lp_ref, sg_ref, q_ref, k_ref, vt_ref, o_ref, *, lam_init):
    q0, q1 = _masked_queries(q_ref[...])
    qs = jnp.concatenate([q0, q1], axis=0)
    s = lax.dot_general(k_ref[...], qs, (((1,), (1,)), ((), ())), preferred_element_type=F32)
    p = jnp.exp2(s - jnp.max(s, axis=0, keepdims=True))
    l = jnp.sum(p, axis=0, keepdims=True)
    acc = _dot(vt_ref[...], p.astype(BF16))
    o_ref[...] = _attn_epilogue(acc, l, _lambda(lp_ref[...], lam_init), sg_ref[...], lam_init)


def _attention(q, k, vt, lam_params_l, subln_g_l, *, lam_init, with_ctx_queries):
    ctx_blk0 = R_LAT // CTX_LEN
    sg = subln_g_l.reshape(V_DIM, 1)
    attn = pl.pallas_call(
        functools.partial(_attn_kernel, lam_init=lam_init),
        grid=(BATCH, N_HEADS),
        in_specs=[
            pl.BlockSpec((4, HEAD_DIM), lambda b, h: (0, 0)),
            pl.BlockSpec((V_DIM, 1), lambda b, h: (0, 0)),
            pl.BlockSpec((SEQ, V_DIM), lambda b, h: (b, h)),
            pl.BlockSpec((CTX_LEN, V_DIM), lambda b, h: (ctx_blk0 + b, h)),
            pl.BlockSpec((V_DIM, CTX_LEN), lambda b, h: (h, ctx_blk0 + b)),
            pl.BlockSpec((SEQ, V_DIM), lambda b, h: (b, h)),
            pl.BlockSpec((V_DIM, SEQ), lambda b, h: (h, b)),
        ],
        out_specs=pl.BlockSpec((SEQ, V_DIM), lambda b, h: (b, h)),
        out_shape=jax.ShapeDtypeStruct((R_LAT, ATTN_WIDTH), BF16),
        scratch_shapes=[
            pltpu.VMEM((2 * TQ, V_DIM), BF16),
            pltpu.VMEM((V_DIM, 2 * TQ), F32),
            pltpu.VMEM((8, 2 * TQ), F32),
            pltpu.VMEM((2 * TQ, V_DIM), BF16),
            pltpu.VMEM((1, 2 * TQ), F32),
            pltpu.VMEM((8, 2 * TQ), F32),
            pltpu.VMEM((V_DIM, 2 * TQ), F32),
        ],
        compiler_params=pltpu.CompilerParams(
            dimension_semantics=("arbitrary", "arbitrary"), vmem_limit_bytes=VMEM_LIMIT),
        name="diff_attn",
    )(lam_params_l, sg, q, k, vt, k, vt)
    if not with_ctx_queries:
        return attn, None
    attn_ctx = pl.pallas_call(
        functools.partial(_attn_ctx_kernel, lam_init=lam_init),
        grid=(BATCH, N_HEADS),
        in_specs=[
            pl.BlockSpec((4, HEAD_DIM), lambda b, h: (0, 0)),
            pl.BlockSpec((V_DIM, 1), lambda b, h: (0, 0)),
            pl.BlockSpec((CTX_LEN, V_DIM), lambda b, h: (ctx_blk0 + b, h)),
            pl.BlockSpec((CTX_LEN, V_DIM), lambda b, h: (ctx_blk0 + b, h)),
            pl.BlockSpec((V_DIM, CTX_LEN), lambda b, h: (h, ctx_blk0 + b)),
        ],
        out_specs=pl.BlockSpec((CTX_LEN, V_DIM), lambda b, h: (b, h)),
        out_shape=jax.ShapeDtypeStruct((BATCH * CTX_LEN, ATTN_WIDTH), BF16),
        compiler_params=pltpu.CompilerParams(
            dimension_semantics=("arbitrary", "arbitrary"), vmem_limit_bytes=VMEM_LIMIT),
        name="diff_attn_ctx",
    )(lam_params_l, sg, q, k, vt)
    return attn, attn_ctx


def _dft1_kernel(x_ref, w_ref, twr_ref, twi_ref, o_ref):
    n = DFT_A * NS2
    x = x_ref[...].reshape(n, 2 * FOURIER_WIDTH)
    xs = jnp.concatenate([x[:, :FOURIER_WIDTH], x[:, FOURIER_WIDTH:]], axis=0)
    r = _dot(w_ref[...], xs)
    re, im = r[:n], r[n:]
    twr, twi = twr_ref[0], twi_ref[0]
    o_ref[:, :, :FOURIER_WIDTH] = (re * twr - im * twi).astype(BF16).reshape(DFT_A, NS2, FOURIER_WIDTH)
    o_ref[:, :, FOURIER_WIDTH:] = (re * twi + im * twr).astype(BF16).reshape(DFT_A, NS2, FOURIER_WIDTH)


def _dft2_kernel(x_ref, w_ref, o_ref, *, n, nk, scale):
    g = n // DFT_B
    for j in range(nk):
        x = x_ref[j * g:(j + 1) * g].reshape(n, 2 * FOURIER_WIDTH)
        xs = jnp.concatenate([x[:, :FOURIER_WIDTH], x[:, FOURIER_WIDTH:]], axis=0)
        r = _dot(w_ref[...], xs) * scale
        o_ref[:, j * FOURIER_WIDTH:(j + 1) * FOURIER_WIDTH] = r.astype(BF16)


def _fourier_latent(f_all, t1, w2):
    w1, twr, twi = t1
    blk = (DFT_A, NS2, 2 * FOURIER_WIDTH)
    n = DFT_A * NS2
    x2 = pl.pallas_call(
        _dft1_kernel,
        grid=(BATCH, DFT_B // NS2),
        in_specs=[
            pl.BlockSpec(blk, lambda b, j: (b, j, 0)),
            pl.BlockSpec((2 * n, 2 * n), lambda b, j: (0, 0)),
            pl.BlockSpec((1, n, 1), lambda b, j: (j, 0, 0)),
            pl.BlockSpec((1, n, 1), lambda b, j: (j, 0, 0)),
        ],
        out_specs=pl.BlockSpec(blk, lambda b, j: (b, j, 0)),
        out_shape=jax.ShapeDtypeStruct((BATCH * DFT_A, DFT_B, 2 * FOURIER_WIDTH), BF16),
        compiler_params=pltpu.CompilerParams(
            dimension_semantics=("arbitrary", "arbitrary"), vmem_limit_bytes=VMEM_LIMIT),
        name="dft_stage1",
    )(f_all, w1, twr, twi)
    y = pl.pallas_call(
        functools.partial(_dft2_kernel, n=DFT_B, nk=NK1, scale=(SEQ * FGROUP_DIM) ** -0.5),
        grid=(BATCH, DFT_A // NK1),
        in_specs=[
            pl.BlockSpec((NK1, DFT_B, 2 * FOURIER_WIDTH), lambda b, j: (b * (DFT_A // NK1) + j, 0, 0)),
            pl.BlockSpec((DFT_B, 2 * DFT_B), lambda b, j: (0, 0)),
        ],
        out_specs=pl.BlockSpec((DFT_B, NK1 * FOURIER_WIDTH), lambda b, j: (b, j)),
        out_shape=jax.ShapeDtypeStruct((R_LAT // DFT_A, DFT_A * FOURIER_WIDTH), BF16),
        compiler_params=pltpu.CompilerParams(
            dimension_semantics=("arbitrary", "arbitrary"), vmem_limit_bytes=VMEM_LIMIT),
        name="dft_stage2",
    )(x2, w2)
    return y.reshape(R_LAT, FOURIER_WIDTH)


def _fourier_ctx(f_all, wctx):
    ctx_blk0 = R_LAT // CTX_LEN
    return pl.pallas_call(
        functools.partial(_dft2_kernel, n=CTX_LEN, nk=1, scale=(CTX_LEN * FGROUP_DIM) ** -0.5),
        grid=(BATCH,),
        in_specs=[
            pl.BlockSpec((CTX_LEN // DFT_B, DFT_B, 2 * FOURIER_WIDTH), lambda b: (ctx_blk0 + b, 0, 0)),
            pl.BlockSpec((CTX_LEN, 2 * CTX_LEN), lambda b: (0, 0)),
        ],
        out_specs=pl.BlockSpec((CTX_LEN, FOURIER_WIDTH), lambda b: (b, 0)),
        out_shape=jax.ShapeDtypeStruct((BATCH * CTX_LEN, FOURIER_WIDTH), BF16),
        compiler_params=pltpu.CompilerParams(
            dimension_semantics=("arbitrary",), vmem_limit_bytes=VMEM_LIMIT),
        name="dft_ctx",
    )(f_all, wctx)


def _merge_kernel(*refs, split_input):
    refs = list(refs)
    x_ref, m_ref, a_ref, f_ref = refs[:4]
    ac_ref, fc_ref = (refs[4], refs[5]) if split_input else (None, None)
    ga_ref, gf_ref, wa_ref, wf_ref, wo_ref, o_ref = refs[-6:]
    attn, four = a_ref[...], f_ref[...]
    if split_input:
        latent = pl.program_id(0) < N_LAT_TILES
        attn = jnp.where(latent, attn, ac_ref[...])
        four = jnp.where(latent, four, fc_ref[...])
    a = _dot(attn, wa_ref[0])
    fr = _dot(four, wf_ref[0])
    merged = ga_ref[...].astype(F32) * a + gf_ref[...].astype(F32) * fr
    mix = _dot(merged.astype(BF16), wo_ref[0])
    o_ref[...] = x_ref[...] + m_ref[0, 0][5:6] * mix


def _merge(x, m_l, attn, four, ga, gf, wa, wf, wo, *, layer, n_tiles, attn_ctx=None, four_ctx=None):
    split_input = attn_ctx is not None
    const = dict(pipeline_mode=pl.Buffered(1))
    row = lambda i: (i, 0)
    lat_row = (lambda i: (jnp.minimum(i, N_LAT_TILES - 1), 0)) if split_input else row
    fix = lambda i: (0, 0)
    wsel = lambda i: (layer, 0, 0)
    in_specs = [
        pl.BlockSpec((TM, D_MODEL), row),
        pl.BlockSpec((1, 1, N_MOD, D_MODEL), lambda i: (0, _mod_row(i), 0, 0)),
        pl.BlockSpec((TM, ATTN_WIDTH), lat_row),
        pl.BlockSpec((TM, FOURIER_WIDTH), lat_row),
    ]
    args = [x, m_l, attn, four]
    if split_input:
        in_specs += [pl.BlockSpec((TM, ATTN_WIDTH), fix), pl.BlockSpec((TM, FOURIER_WIDTH), fix)]
        args += [attn_ctx, four_ctx]
    in_specs += [
        pl.BlockSpec((TM, D_MODEL), row),
        pl.BlockSpec((TM, D_MODEL), row),
        pl.BlockSpec((1, ATTN_WIDTH, D_MODEL), wsel, **const),
        pl.BlockSpec((1, FOURIER_WIDTH, D_MODEL), wsel, **const),
        pl.BlockSpec((1, D_MODEL, D_MODEL), wsel, **const),
    ]
    args += [ga, gf, wa, wf, wo]
    return pl.pallas_call(
        functools.partial(_merge_kernel, split_input=split_input),
        grid=(n_tiles,),
        in_specs=in_specs,
        out_specs=pl.BlockSpec((TM, D_MODEL), row),
        out_shape=jax.ShapeDtypeStruct((n_tiles * TM, D_MODEL), F32),
        compiler_params=pltpu.CompilerParams(
            dimension_semantics=("arbitrary",), vmem_limit_bytes=VMEM_LIMIT),
        name="merge",
    )(*args)


def _rope_tables():
    lane = jnp.arange(128, dtype=jnp.int32) % HEAD_DIM
    inv = 1.0 / (ROPE_BASE ** ((2 * (lane % (AXIS_DIM // 2))).astype(F32) / AXIS_DIM))
    row_lane = lane < AXIS_DIM
    first = (lane % AXIS_DIM) < (AXIS_DIM // 2)

    def tables(pos, lanes):
        ang = pos.astype(F32)[:, None] * inv[None, :]
        cos, sin = jnp.cos(ang), jnp.sin(ang)
        tabs = jnp.stack([cos, jnp.where(first[None, :], -sin, 0.0), jnp.where(first[None, :], 0.0, sin)])
        return jnp.where(lanes[None, None, :], tabs, 0.0)

    identity = jnp.stack([jnp.ones((128,), F32), jnp.zeros((128,), F32), jnp.zeros((128,), F32)])[:, None, :]
    lat_rows = tables(jnp.arange(SEQ // GRID_W, dtype=jnp.int32), row_lane)
    ctx_rows = jnp.broadcast_to(jnp.where(row_lane[None, None, :], identity, 0.0),
                                (3, BATCH * CTX_LEN // GRID_W, 128))
    rope_rows = jnp.concatenate([lat_rows] * BATCH + [ctx_rows], axis=1)
    lat_cols = tables(jnp.arange(TM, dtype=jnp.int32) % GRID_W, ~row_lane)
    ctx_cols = jnp.broadcast_to(jnp.where(~row_lane[None, None, :], identity, 0.0), (3, TM, 128))
    return rope_rows, jnp.stack([lat_cols, ctx_cols])


def _angles(num, den):
    return (2.0 * math.pi / den) * (num % den).astype(F32)


def _dft_tables():
    c = jnp.arange(FGROUP_DIM, dtype=jnp.int32)
    ang = _angles(c[:, None] * c[None, :], FGROUP_DIM)
    wc = jnp.concatenate([jnp.cos(ang), -jnp.sin(ang)], axis=1).astype(BF16)

    n = DFT_A * NS2
    a = jnp.arange(DFT_A, dtype=jnp.int32)
    ang = _angles(a[:, None] * a[None, :], DFT_A)
    cs, sn = jnp.cos(ang), jnp.sin(ang)
    m1 = jnp.concatenate([jnp.concatenate([cs, sn], axis=1),
                          jnp.concatenate([-sn, cs], axis=1)], axis=0)
    idx = jnp.arange(2 * n, dtype=jnp.int32)
    spread = (idx[:, None] // NS2 == jnp.arange(2 * DFT_A, dtype=jnp.int32)[None, :]).astype(F32)
    same_pos = idx[:, None] % NS2 == idx[None, :] % NS2
    w1 = jnp.where(same_pos, spread @ m1 @ spread.T, 0.0).astype(BF16)
    s2 = jnp.arange(DFT_B, dtype=jnp.int32).reshape(DFT_B // NS2, 1, NS2)
    ang = _angles(a[None, :, None] * s2, SEQ).reshape(DFT_B // NS2, n, 1)
    t1 = (w1, jnp.cos(ang), -jnp.sin(ang))

    b = jnp.arange(DFT_B, dtype=jnp.int32)
    ang = _angles(b[:, None] * b[None, :], DFT_B)
    w2 = jnp.concatenate([jnp.cos(ang), jnp.sin(ang)], axis=1).astype(BF16)

    n = jnp.arange(CTX_LEN, dtype=jnp.int32)
    ang = _angles(n[:, None] * n[None, :], CTX_LEN)
    wctx = jnp.concatenate([jnp.cos(ang), jnp.sin(ang)], axis=1).astype(BF16)
    return wc, t1, w2, wctx


def kernel(x, c, ctx, c_ctx, w_mod, b_mod, norm_g, ffn_w_in, ffn_w_out, w_in, b_gate,
           lam_params, subln_g, w_attn_out, w_four_out, w_o, final_g):
    xs = x.reshape(R_LAT, D_MODEL)
    x_ctx = ctx.reshape(BATCH * CTX_LEN, D_MODEL)
    conds = jnp.concatenate([c, c_ctx[None, :], jnp.zeros((8 - BATCH - 1, D_MODEL), F32)], axis=0)
    m_all = _modulation(conds, w_mod, b_mod)
    tabs = _rope_tables()
    wc, t1, w2, wctx = _dft_tables()

    w_ffn_in, w_ffn_out = ffn_w_in.astype(BF16), ffn_w_out.astype(BF16)
    w_in_b = w_in.astype(BF16)
    wa, wf, wo = w_attn_out.astype(BF16), w_four_out.astype(BF16), w_o.astype(BF16)

    for l in range(DEPTH):
        last = l == DEPTH - 1
        lam_init = 0.8 - 0.6 * math.exp(-0.3 * l)
        m_l = m_all[l:l + 1]
        xs = _ffn(xs, m_l, norm_g[l, 0], w_ffn_in, w_ffn_out, wsel=(l, 0), sub=0, n_tiles=N_ALL_TILES,
                  x_ctx=x_ctx if l == 0 else None)
        q, k, vt, f, ga, gf = _proj(xs, m_l, norm_g[l, 1], w_in_b, tabs, wc, b_gate[l], layer=l)
        attn, attn_ctx = _attention(q, k, vt, lam_params[l], subln_g[l], lam_init=lam_init,
                                    with_ctx_queries=not last)
        four = _fourier_latent(f, t1, w2)
        four_ctx = None if last else _fourier_ctx(f, wctx)
        n_tiles = N_LAT_TILES if last else N_ALL_TILES
        xs = _merge(xs, m_l, attn, four, ga, gf, wa, wf, wo, layer=l, n_tiles=n_tiles,
                    attn_ctx=attn_ctx, four_ctx=four_ctx)
        xs = _ffn(xs, m_l, norm_g[l, 2], w_ffn_in, w_ffn_out, wsel=(l, 1), sub=2, n_tiles=n_tiles,
                  final_g=final_g if last else None)
    return xs.reshape(BATCH, SEQ, D_MODEL)
```

```python
import functools
import math

import jax
import jax.numpy as jnp
from jax import lax
from jax.experimental import pallas as pl
from jax.experimental.pallas import tpu as pltpu

D_MODEL = 1024
BATCH = 2
SEQ = 8192
DEPTH = 2
CTX_LEN = 256
GRID_W = 64
N_HEADS = 8
HEAD_DIM = 64
V_DIM = 2 * HEAD_DIM
QK_W = N_HEADS * 2 * HEAD_DIM
ATTN_WIDTH = N_HEADS * V_DIM
N_FGROUPS = 4
FGROUP_DIM = 128
FOURIER_WIDTH = N_FGROUPS * FGROUP_DIM
F_OFF = 2 * QK_W + ATTN_WIDTH
G_OFF = F_OFF + FOURIER_WIDTH
IN_WIDTH = G_OFF + 2 * D_MODEL
D_FF = 2816
AXIS_DIM = HEAD_DIM // 2
ROPE_BASE = 10000.0
N_MOD = 9
EPS = 1e-6

R_LAT = BATCH * SEQ
R_ALL = R_LAT + BATCH * CTX_LEN
TM = 512
N_LAT_TILES = R_LAT // TM
N_ALL_TILES = R_ALL // TM
FF_CHUNK = 256
N_FF_CHUNKS = D_FF // FF_CHUNK
TQ = 256
TK = 4096
N_LAT_BLOCKS = SEQ // TK
PROB_SUM_LIMIT = 2.0 ** 24
N_LAT_QT = SEQ // TQ
DFT_A = 64
DFT_B = 128
NS2 = 16
NK1 = 8
VMEM_LIMIT = 56 * 1024 * 1024

F32 = jnp.float32
BF16 = jnp.bfloat16


def _dot(a, b):
    return jnp.dot(a, b, preferred_element_type=F32)


def _sigmoid(x):
    return 1.0 / (1.0 + jnp.exp(-x))


def _mod_row(i):
    return jnp.minimum(i // (SEQ // TM), 2)


def _norm_mod(x, g, shift, scale):
    var = jnp.mean(x * x, axis=-1, keepdims=True)
    return x * lax.rsqrt(var + EPS) * (g * (1.0 + scale)) + shift


def _mod_kernel(c_ref, w_ref, b_ref, o_ref):
    c = c_ref[...]
    s = c * _sigmoid(c)
    o_ref[0] = jnp.dot(s, w_ref[0], preferred_element_type=F32,
                       precision=lax.Precision.HIGHEST) + b_ref[0]


def _modulation(conds, w_mod, b_mod):
    tn = D_MODEL
    out = pl.pallas_call(
        _mod_kernel,
        grid=(DEPTH, N_MOD),
        in_specs=[
            pl.BlockSpec((8, D_MODEL), lambda l, j: (0, 0)),
            pl.BlockSpec((1, D_MODEL, tn), lambda l, j: (l, 0, j)),
            pl.BlockSpec((1, 1, tn), lambda l, j: (l, 0, j)),
        ],
        out_specs=pl.BlockSpec((1, 8, tn), lambda l, j: (l, 0, j)),
        out_shape=jax.ShapeDtypeStruct((DEPTH, 8, N_MOD * D_MODEL), F32),
        compiler_params=pltpu.CompilerParams(
            dimension_semantics=("arbitrary", "arbitrary"), vmem_limit_bytes=VMEM_LIMIT),
        name="modulation",
    )(conds, w_mod, b_mod.reshape(DEPTH, 1, N_MOD * D_MODEL))
    return out.reshape(DEPTH, 8, N_MOD, D_MODEL)


def _ffn_kernel(*refs, sub, final, split_input):
    refs = list(refs)
    x_ref = refs.pop(0)
    xc_ref = refs.pop(0) if split_input else None
    m_ref, g_ref, win_ref, wout_ref = refs[:4]
    fg_ref = refs[4] if final else None
    o_ref, u_ref, h_ref = refs[-3:]
    x = x_ref[...]
    if split_input:
        x = jnp.where(pl.program_id(0) < N_LAT_TILES, x, xc_ref[...])
    m = m_ref[0, 0]
    shift = m[3 * sub:3 * sub + 1]
    scale = m[3 * sub + 1:3 * sub + 2]
    gate = m[3 * sub + 2:3 * sub + 3]
    u_ref[...] = _norm_mod(x, g_ref[...], shift, scale).astype(BF16)
    for c in range(N_FF_CHUNKS):
        lo = c * FF_CHUNK
        a = _dot(u_ref[...], win_ref[0, 0, :, lo:lo + FF_CHUNK])
        b = _dot(u_ref[...], win_ref[0, 0, :, D_FF + lo:D_FF + lo + FF_CHUNK])
        h_ref[:, lo:lo + FF_CHUNK] = (a * _sigmoid(a) * b).astype(BF16)
    y = x + 0.5 * gate * _dot(h_ref[...], wout_ref[0, 0])
    if final:
        var = jnp.mean(y * y, axis=-1, keepdims=True)
        y = y * lax.rsqrt(var + EPS) * fg_ref[...]
    o_ref[...] = y


def _ffn(x, m_l, g, w_in_c, w_out_c, *, wsel, sub, n_tiles, final_g=None, x_ctx=None):
    final = final_g is not None
    split_input = x_ctx is not None
    const = dict(pipeline_mode=pl.Buffered(1))
    if split_input:
        in_specs = [pl.BlockSpec((TM, D_MODEL), lambda i: (jnp.minimum(i, N_LAT_TILES - 1), 0)),
                    pl.BlockSpec((TM, D_MODEL), lambda i: (0, 0))]
        args = [x, x_ctx]
    else:
        in_specs = [pl.BlockSpec((TM, D_MODEL), lambda i: (i, 0))]
        args = [x]
    in_specs += [
        pl.BlockSpec((1, 1, N_MOD, D_MODEL), lambda i: (0, _mod_row(i), 0, 0)),
        pl.BlockSpec((1, D_MODEL), lambda i: (0, 0)),
        pl.BlockSpec((1, 1, D_MODEL, 2 * D_FF), lambda i: wsel + (0, 0), **const),
        pl.BlockSpec((1, 1, D_FF, D_MODEL), lambda i: wsel + (0, 0), **const),
    ]
    args += [m_l, g.reshape(1, D_MODEL), w_in_c, w_out_c]
    if final:
        in_specs.append(pl.BlockSpec((1, D_MODEL), lambda i: (0, 0)))
        args.append(final_g.reshape(1, D_MODEL))
    return pl.pallas_call(
        functools.partial(_ffn_kernel, sub=sub, final=final, split_input=split_input),
        grid=(n_tiles,),
        in_specs=in_specs,
        out_specs=pl.BlockSpec((TM, D_MODEL), lambda i: (i, 0)),
        out_shape=jax.ShapeDtypeStruct((n_tiles * TM, D_MODEL), F32),
        scratch_shapes=[pltpu.VMEM((TM, D_MODEL), BF16), pltpu.VMEM((TM, D_FF), BF16)],
        compiler_params=pltpu.CompilerParams(
            dimension_semantics=("arbitrary",), vmem_limit_bytes=VMEM_LIMIT),
        name="ffn",
    )(*args)


def _proj_kernel(x_ref, m_ref, g_ref, w_ref, rrow_ref, rcol_ref, wc_ref, bg_ref,
                 q_ref, k_ref, vt_ref, f_ref, ga_ref, gf_ref, u_ref):
    x = x_ref[...]
    m = m_ref[0, 0]
    u_ref[...] = _norm_mod(x, g_ref[...], m[3:4], m[4:5]).astype(BF16)
    cw = 256

    def rope_table(t):
        per_row = rrow_ref[t]
        rows = jnp.broadcast_to(per_row[:, None, :], (TM // GRID_W, GRID_W, 128)).reshape(TM, 128)
        tab = rows + rcol_ref[0, t]
        return jnp.concatenate([tab] * (cw // 128), axis=1)

    cos, sa, sb = rope_table(0), rope_table(1), rope_table(2)

    def col(off):
        return _dot(u_ref[...], w_ref[0, :, off:off + cw])

    for out_ref, base, qscale in ((q_ref, 0, HEAD_DIM ** -0.5 * math.log2(math.e)), (k_ref, QK_W, 1.0)):
        for j in range(QK_W // cw):
            p = col(base + j * cw)
            r = p * cos + pltpu.roll(p, cw - 16, 1) * sa + pltpu.roll(p, 16, 1) * sb
            if qscale != 1.0:
                r = r * qscale
            out_ref[:, j * cw:(j + 1) * cw] = r.astype(BF16)
    for j in range(ATTN_WIDTH // cw):
        vt_ref[j * cw:(j + 1) * cw, :] = col(2 * QK_W + j * cw).T.astype(BF16)
    for j in range(FOURIER_WIDTH // cw):
        p = col(F_OFF + j * cw).astype(BF16)
        for t in range(cw // FGROUP_DIM):
            grp = j * (cw // FGROUP_DIM) + t
            r = _dot(p[:, t * FGROUP_DIM:(t + 1) * FGROUP_DIM], wc_ref[...])
            shape3 = (TM // DFT_B, DFT_B, FGROUP_DIM)
            f_ref[:, :, grp * FGROUP_DIM:(grp + 1) * FGROUP_DIM] = r[:, :FGROUP_DIM].astype(BF16).reshape(shape3)
            f_ref[:, :, FOURIER_WIDTH + grp * FGROUP_DIM:FOURIER_WIDTH + (grp + 1) * FGROUP_DIM] = (
                r[:, FGROUP_DIM:].astype(BF16).reshape(shape3))
    for out_ref, base, brow in ((ga_ref, G_OFF, 0), (gf_ref, G_OFF + D_MODEL, 1)):
        for j in range(D_MODEL // cw):
            p = col(base + j * cw) + bg_ref[brow:brow + 1, j * cw:(j + 1) * cw]
            out_ref[:, j * cw:(j + 1) * cw] = _sigmoid(p).astype(BF16)


def _proj(x, m_l, g, w_in_b, tabs, wc, b_gate, *, layer):
    rope_rows, rope_cols = tabs
    const = dict(pipeline_mode=pl.Buffered(1))
    row = lambda i: (i, 0)
    fix = lambda i: (0, 0)
    wide = jax.ShapeDtypeStruct((R_ALL, D_MODEL), BF16)
    row_spec = pl.BlockSpec((TM, D_MODEL), row)
    return pl.pallas_call(
        _proj_kernel,
        grid=(N_ALL_TILES,),
        in_specs=[
            pl.BlockSpec((TM, D_MODEL), row),
            pl.BlockSpec((1, 1, N_MOD, D_MODEL), lambda i: (0, _mod_row(i), 0, 0)),
            pl.BlockSpec((1, D_MODEL), fix),
            pl.BlockSpec((1, D_MODEL, IN_WIDTH), lambda i: (layer, 0, 0), **const),
            pl.BlockSpec((3, TM // GRID_W, 128), lambda i: (0, i, 0)),
            pl.BlockSpec((1, 3, TM, 128), lambda i: (i // N_LAT_TILES, 0, 0, 0)),
            pl.BlockSpec((FGROUP_DIM, 2 * FGROUP_DIM), fix),
            pl.BlockSpec((2, D_MODEL), fix),
        ],
        out_specs=[row_spec, row_spec, pl.BlockSpec((ATTN_WIDTH, TM), lambda i: (0, i)),
                   pl.BlockSpec((TM // DFT_B, DFT_B, 2 * FOURIER_WIDTH), lambda i: (i, 0, 0)), row_spec, row_spec],
        out_shape=[wide, wide, jax.ShapeDtypeStruct((ATTN_WIDTH, R_ALL), BF16),
                   jax.ShapeDtypeStruct((R_ALL // DFT_B, DFT_B, 2 * FOURIER_WIDTH), BF16), wide, wide],
        scratch_shapes=[pltpu.VMEM((TM, D_MODEL), BF16)],
        compiler_params=pltpu.CompilerParams(
            dimension_semantics=("arbitrary",), vmem_limit_bytes=VMEM_LIMIT),
        name="proj",
    )(x, m_l, g.reshape(1, D_MODEL), w_in_b, rope_rows, rope_cols, wc, b_gate)


def _masked_queries(q):
    lane = lax.broadcasted_iota(jnp.int32, q.shape, 1)
    zero = jnp.zeros_like(q)
    return jnp.where(lane < HEAD_DIM, q, zero), jnp.where(lane >= HEAD_DIM, q, zero)


def _lambda(lp, lam_init):
    return (jnp.exp(jnp.sum(lp[0:1] * lp[1:2], axis=-1, keepdims=True))
            - jnp.exp(jnp.sum(lp[2:3] * lp[3:4], axis=-1, keepdims=True)) + lam_init)


def _attn_epilogue(acc, l, lam, sg, lam_init):
    n = acc.shape[1] // 2
    o = acc / l
    d = o[:, :n] - lam * o[:, n:]
    var = jnp.mean(d * d, axis=0, keepdims=True)
    y = d * lax.rsqrt(var + EPS) * sg * (1.0 - lam_init)
    return y.T.astype(BF16)


def _attn_kernel(lp_ref, sg_ref, q_ref, kc_ref, vc_ref, kl_ref, vl_ref, o_ref,
                 qs_ref, pacc_ref, pl_ref, qx_ref, m_ref, l_ref, acc_ref, *, lam_init):
    lam = _lambda(lp_ref[...], lam_init)

    def tile_rows(t):
        if isinstance(t, int):
            return pl.ds(t * TQ, TQ)
        return pl.ds(pl.multiple_of(t * TQ, TQ), TQ)

    def stage_queries(dst, t):
        q0, q1 = _masked_queries(q_ref[tile_rows(t), :])
        dst[0:TQ, :] = q0
        dst[TQ:, :] = q1

    def scores(keys, queries_ref):
        return lax.dot_general(keys, queries_ref[...], (((1,), (1,)), ((), ())), preferred_element_type=F32)

    def fold8(fn, a):
        return fn(a.reshape(a.shape[0] // 8, 8, 2 * TQ), axis=0)

    def latent_block(blk):
        return kl_ref[blk * TK:(blk + 1) * TK, :], vl_ref[:, blk * TK:(blk + 1) * TK]

    def store_tile(t, acc, l8):
        l = jnp.sum(l8, axis=0, keepdims=True)
        o_ref[tile_rows(t), :] = _attn_epilogue(acc, l, lam, sg_ref[...], lam_init)

    def exact_tile(t):
        stage_queries(qx_ref, t)
        m_ref[...] = jnp.full_like(m_ref, -jnp.inf)
        l_ref[...] = jnp.zeros_like(l_ref)
        acc_ref[...] = jnp.zeros_like(acc_ref)

        def update(keys, vt):
            s = scores(keys, qx_ref)
            m_old = m_ref[...]
            m_new = jnp.maximum(m_old, jnp.max(s, axis=0, keepdims=True))
            alpha = jnp.exp2(m_old - m_new)
            p = jnp.exp2(s - m_new)
            m_ref[...] = m_new
            l_ref[...] = alpha * l_ref[...] + fold8(jnp.sum, p)
            acc_ref[...] = alpha * acc_ref[...] + _dot(vt, p.astype(BF16))

        def latent(blk, carry):
            off = pl.multiple_of(blk * TK, TK)
            update(kl_ref[pl.ds(off, TK), :], vl_ref[:, pl.ds(off, TK)])
            return carry

        update(kc_ref[...], vc_ref[...])
        lax.fori_loop(0, N_LAT_BLOCKS, latent, 0)
        store_tile(t, acc_ref[...], l_ref[...])

    def tile(t, redo_prev):
        stage_queries(qs_ref, t)
        s = scores(kc_ref[...], qs_ref)
        m0 = jnp.max(s, axis=0, keepdims=True)
        p = jnp.exp2(s - m0)
        l8 = fold8(jnp.sum, p)
        acc = _dot(vc_ref[...], p.astype(BF16))
        for blk in range(N_LAT_BLOCKS):
            keys, vt = latent_block(blk)
            p = jnp.exp2(scores(keys, qs_ref) - m0)
            l8 = l8 + fold8(jnp.sum, p)
            acc = acc + _dot(vt, p.astype(BF16))
        prev = jnp.maximum(t - 1, 0)
        store_tile(prev, pacc_ref[...], pl_ref[...])
        pacc_ref[...] = acc
        pl_ref[...] = l8
        redo = (jnp.max(l8) > PROB_SUM_LIMIT).astype(jnp.int32)

        @pl.when(redo_prev == 1)
        def _():
            exact_tile(prev)

        return redo

    pacc_ref[...] = jnp.ones_like(pacc_ref)
    pl_ref[...] = jnp.ones_like(pl_ref)
    redo_last = lax.fori_loop(0, N_LAT_QT, tile, jnp.int32(0))
    store_tile(N_LAT_QT - 1, pacc_ref[...], pl_ref[...])

    @pl.when(redo_last == 1)
    def _():
        exact_tile(N_LAT_QT - 1)


def _attn_ctx_kernel(lp_ref, sg_ref, q_ref, k_ref, vt_ref, o_ref, *, lam_init):
    q0, q1 = _masked_queries(q_ref[...])
    qs = jnp.concatenate([q0, q1], axis=0)
    s = lax.dot_general(k_ref[...], qs, (((1,), (1,)), ((), ())), preferred_element_type=F32)
    p = jnp.exp2(s - jnp.max(s, axis=0, keepdims=True))
    l = jnp.sum(p, axis=0, keepdims=True)
    acc = _dot(vt_ref[...], p.astype(BF16))
    o_ref[...] = _attn_epilogue(acc, l, _lambda(lp_ref[...], lam_init), sg_ref[...], lam_init)


def _attention(q, k, vt, lam_params_l, subln_g_l, *, lam_init, with_ctx_queries):
    ctx_blk0 = R_LAT // CTX_LEN
    sg = subln_g_l.reshape(V_DIM, 1)
    attn = pl.pallas_call(
        functools.partial(_attn_kernel, lam_init=lam_init),
        grid=(BATCH, N_HEADS),
        in_specs=[
            pl.BlockSpec((4, HEAD_DIM), lambda b, h: (0, 0)),
            pl.BlockSpec((V_DIM, 1), lambda b, h: (0, 0)),
            pl.BlockSpec((SEQ, V_DIM), lambda b, h: (b, h)),
            pl.BlockSpec((CTX_LEN, V_DIM), lambda b, h: (ctx_blk0 + b, h)),
            pl.BlockSpec((V_DIM, CTX_LEN), lambda b, h: (h, ctx_blk0 + b)),
            pl.BlockSpec((SEQ, V_DIM), lambda b, h: (b, h)),
            pl.BlockSpec((V_DIM, SEQ), lambda b, h: (h, b)),
        ],
        out_specs=pl.BlockSpec((SEQ, V_DIM), lambda b, h: (b, h)),
        out_shape=jax.ShapeDtypeStruct((R_LAT, ATTN_WIDTH), BF16),
        scratch_shapes=[
            pltpu.VMEM((2 * TQ, V_DIM), BF16),
            pltpu.VMEM((V_DIM, 2 * TQ), F32),
            pltpu.VMEM((8, 2 * TQ), F32),
            pltpu.VMEM((2 * TQ, V_DIM), BF16),
            pltpu.VMEM((1, 2 * TQ), F32),
            pltpu.VMEM((8, 2 * TQ), F32),
            pltpu.VMEM((V_DIM, 2 * TQ), F32),
        ],
        compiler_params=pltpu.CompilerParams(
            dimension_semantics=("arbitrary", "arbitrary"), vmem_limit_bytes=VMEM_LIMIT),
        name="diff_attn",
    )(lam_params_l, sg, q, k, vt, k, vt)
    if not with_ctx_queries:
        return attn, None
    attn_ctx = pl.pallas_call(
        functools.partial(_attn_ctx_kernel, lam_init=lam_init),
        grid=(BATCH, N_HEADS),
        in_specs=[
            pl.BlockSpec((4, HEAD_DIM), lambda b, h: (0, 0)),
            pl.BlockSpec((V_DIM, 1), lambda b, h: (0, 0)),
            pl.BlockSpec((CTX_LEN, V_DIM), lambda b, h: (ctx_blk0 + b, h)),
            pl.BlockSpec((CTX_LEN, V_DIM), lambda b, h: (ctx_blk0 + b, h)),
            pl.BlockSpec((V_DIM, CTX_LEN), lambda b, h: (h, ctx_blk0 + b)),
        ],
        out_specs=pl.BlockSpec((CTX_LEN, V_DIM), lambda b, h: (b, h)),
        out_shape=jax.ShapeDtypeStruct((BATCH * CTX_LEN, ATTN_WIDTH), BF16),
        compiler_params=pltpu.CompilerParams(
            dimension_semantics=("arbitrary", "arbitrary"), vmem_limit_bytes=VMEM_LIMIT),
        name="diff_attn_ctx",
    )(lam_params_l, sg, q, k, vt)
    return attn, attn_ctx


def _dft1_kernel(x_ref, w_ref, twr_ref, twi_ref, o_ref):
    n = DFT_A * NS2
    x = x_ref[...].reshape(n, 2 * FOURIER_WIDTH)
    xs = jnp.concatenate([x[:, :FOURIER_WIDTH], x[:, FOURIER_WIDTH:]], axis=0)
    r = _dot(w_ref[...], xs)
    re, im = r[:n], r[n:]
    twr, twi = twr_ref[0], twi_ref[0]
    o_ref[:, :, :FOURIER_WIDTH] = (re * twr - im * twi).astype(BF16).reshape(DFT_A, NS2, FOURIER_WIDTH)
    o_ref[:, :, FOURIER_WIDTH:] = (re * twi + im * twr).astype(BF16).reshape(DFT_A, NS2, FOURIER_WIDTH)


def _dft2_kernel(x_ref, w_ref, o_ref, *, n, nk, scale):
    g = n // DFT_B
    for j in range(nk):
        x = x_ref[j * g:(j + 1) * g].reshape(n, 2 * FOURIER_WIDTH)
        xs = jnp.concatenate([x[:, :FOURIER_WIDTH], x[:, FOURIER_WIDTH:]], axis=0)
        r = _dot(w_ref[...], xs) * scale
        o_ref[:, j * FOURIER_WIDTH:(j + 1) * FOURIER_WIDTH] = r.astype(BF16)


def _fourier_latent(f_all, t1, w2):
    w1, twr, twi = t1
    blk = (DFT_A, NS2, 2 * FOURIER_WIDTH)
    n = DFT_A * NS2
    x2 = pl.pallas_call(
        _dft1_kernel,
        grid=(BATCH, DFT_B // NS2),
        in_specs=[
            pl.BlockSpec(blk, lambda b, j: (b, j, 0)),
            pl.BlockSpec((2 * n, 2 * n), lambda b, j: (0, 0)),
            pl.BlockSpec((1, n, 1), lambda b, j: (j, 0, 0)),
            pl.BlockSpec((1, n, 1), lambda b, j: (j, 0, 0)),
        ],
        out_specs=pl.BlockSpec(blk, lambda b, j: (b, j, 0)),
        out_shape=jax.ShapeDtypeStruct((BATCH * DFT_A, DFT_B, 2 * FOURIER_WIDTH), BF16),
        compiler_params=pltpu.CompilerParams(
            dimension_semantics=("arbitrary", "arbitrary"), vmem_limit_bytes=VMEM_LIMIT),
        name="dft_stage1",
    )(f_all, w1, twr, twi)
    y = pl.pallas_call(
        functools.partial(_dft2_kernel, n=DFT_B, nk=NK1, scale=(SEQ * FGROUP_DIM) ** -0.5),
        grid=(BATCH, DFT_A // NK1),
        in_specs=[
            pl.BlockSpec((NK1, DFT_B, 2 * FOURIER_WIDTH), lambda b, j: (b * (DFT_A // NK1) + j, 0, 0)),
            pl.BlockSpec((DFT_B, 2 * DFT_B), lambda b, j: (0, 0)),
        ],
        out_specs=pl.BlockSpec((DFT_B, NK1 * FOURIER_WIDTH), lambda b, j: (b, j)),
        out_shape=jax.ShapeDtypeStruct((R_LAT // DFT_A, DFT_A * FOURIER_WIDTH), BF16),
        compiler_params=pltpu.CompilerParams(
            dimension_semantics=("arbitrary", "arbitrary"), vmem_limit_bytes=VMEM_LIMIT),
        name="dft_stage2",
    )(x2, w2)
    return y.reshape(R_LAT, FOURIER_WIDTH)


def _fourier_ctx(f_all, wctx):
    ctx_blk0 = R_LAT // CTX_LEN
    return pl.pallas_call(
        functools.partial(_dft2_kernel, n=CTX_LEN, nk=1, scale=(CTX_LEN * FGROUP_DIM) ** -0.5),
        grid=(BATCH,),
        in_specs=[
            pl.BlockSpec((CTX_LEN // DFT_B, DFT_B, 2 * FOURIER_WIDTH), lambda b: (ctx_blk0 + b, 0, 0)),
            pl.BlockSpec((CTX_LEN, 2 * CTX_LEN), lambda b: (0, 0)),
        ],
        out_specs=pl.BlockSpec((CTX_LEN, FOURIER_WIDTH), lambda b: (b, 0)),
        out_shape=jax.ShapeDtypeStruct((BATCH * CTX_LEN, FOURIER_WIDTH), BF16),
        compiler_params=pltpu.CompilerParams(
            dimension_semantics=("arbitrary",), vmem_limit_bytes=VMEM_LIMIT),
        name="dft_ctx",
    )(f_all, wctx)


def _merge_kernel(*refs, split_input):
    refs = list(refs)
    x_ref, m_ref, a_ref, f_ref = refs[:4]
    ac_ref, fc_ref = (refs[4], refs[5]) if split_input else (None, None)
    ga_ref, gf_ref, wa_ref, wf_ref, wo_ref, o_ref = refs[-6:]
    attn, four = a_ref[...], f_ref[...]
    if split_input:
        latent = pl.program_id(0) < N_LAT_TILES
        attn = jnp.where(latent, attn, ac_ref[...])
        four = jnp.where(latent, four, fc_ref[...])
    a = _dot(attn, wa_ref[0])
    fr = _dot(four, wf_ref[0])
    merged = ga_ref[...].astype(F32) * a + gf_ref[...].astype(F32) * fr
    mix = _dot(merged.astype(BF16), wo_ref[0])
    o_ref[...] = x_ref[...] + m_ref[0, 0][5:6] * mix


def _merge(x, m_l, attn, four, ga, gf, wa, wf, wo, *, layer, n_tiles, attn_ctx=None, four_ctx=None):
    split_input = attn_ctx is not None
    const = dict(pipeline_mode=pl.Buffered(1))
    row = lambda i: (i, 0)
    lat_row = (lambda i: (jnp.minimum(i, N_LAT_TILES - 1), 0)) if split_input else row
    fix = lambda i: (0, 0)
    wsel = lambda i: (layer, 0, 0)
    in_specs = [
        pl.BlockSpec((TM, D_MODEL), row),
        pl.BlockSpec((1, 1, N_MOD, D_MODEL), lambda i: (0, _mod_row(i), 0, 0)),
        pl.BlockSpec((TM, ATTN_WIDTH), lat_row),
        pl.BlockSpec((TM, FOURIER_WIDTH), lat_row),
    ]
    args = [x, m_l, attn, four]
    if split_input:
        in_specs += [pl.BlockSpec((TM, ATTN_WIDTH), fix), pl.BlockSpec((TM, FOURIER_WIDTH), fix)]
        args += [attn_ctx, four_ctx]
    in_specs += [
        pl.BlockSpec((TM, D_MODEL), row),
        pl.BlockSpec((TM, D_MODEL), row),
        pl.BlockSpec((1, ATTN_WIDTH, D_MODEL), wsel, **const),
        pl.BlockSpec((1, FOURIER_WIDTH, D_MODEL), wsel, **const),
        pl.BlockSpec((1, D_MODEL, D_MODEL), wsel, **const),
    ]
    args += [ga, gf, wa, wf, wo]
    return pl.pallas_call(
        functools.partial(_merge_kernel, split_input=split_input),
        grid=(n_tiles,),
        in_specs=in_specs,
        out_specs=pl.BlockSpec((TM, D_MODEL), row),
        out_shape=jax.ShapeDtypeStruct((n_tiles * TM, D_MODEL), F32),
        compiler_params=pltpu.CompilerParams(
            dimension_semantics=("arbitrary",), vmem_limit_bytes=VMEM_LIMIT),
        name="merge",
    )(*args)


def _rope_tables():
    lane = jnp.arange(128, dtype=jnp.int32) % HEAD_DIM
    inv = 1.0 / (ROPE_BASE ** ((2 * (lane % (AXIS_DIM // 2))).astype(F32) / AXIS_DIM))
    row_lane = lane < AXIS_DIM
    first = (lane % AXIS_DIM) < (AXIS_DIM // 2)

    def tables(pos, lanes):
        ang = pos.astype(F32)[:, None] * inv[None, :]
        cos, sin = jnp.cos(ang), jnp.sin(ang)
        tabs = jnp.stack([cos, jnp.where(first[None, :], -sin, 0.0), jnp.where(first[None, :], 0.0, sin)])
        return jnp.where(lanes[None, None, :], tabs, 0.0)

    identity = jnp.stack([jnp.ones((128,), F32), jnp.zeros((128,), F32), jnp.zeros((128,), F32)])[:, None, :]
    lat_rows = tables(jnp.arange(SEQ // GRID_W, dtype=jnp.int32), row_lane)
    ctx_rows = jnp.broadcast_to(jnp.where(row_lane[None, None, :], identity, 0.0),
                                (3, BATCH * CTX_LEN // GRID_W, 128))
    rope_rows = jnp.concatenate([lat_rows] * BATCH + [ctx_rows], axis=1)
    lat_cols = tables(jnp.arange(TM, dtype=jnp.int32) % GRID_W, ~row_lane)
    ctx_cols = jnp.broadcast_to(jnp.where(~row_lane[None, None, :], identity, 0.0), (3, TM, 128))
    return rope_rows, jnp.stack([lat_cols, ctx_cols])


def _angles(num, den):
    return (2.0 * math.pi / den) * (num % den).astype(F32)


def _dft_tables():
    c = jnp.arange(FGROUP_DIM, dtype=jnp.int32)
    ang = _angles(c[:, None] * c[None, :], FGROUP_DIM)
    wc = jnp.concatenate([jnp.cos(ang), -jnp.sin(ang)], axis=1).astype(BF16)

    n = DFT_A * NS2
    a = jnp.arange(DFT_A, dtype=jnp.int32)
    ang = _angles(a[:, None] * a[None, :], DFT_A)
    cs, sn = jnp.cos(ang), jnp.sin(ang)
    m1 = jnp.concatenate([jnp.concatenate([cs, sn], axis=1),
                          jnp.concatenate([-sn, cs], axis=1)], axis=0)
    idx = jnp.arange(2 * n, dtype=jnp.int32)
    spread = (idx[:, None] // NS2 == jnp.arange(2 * DFT_A, dtype=jnp.int32)[None, :]).astype(F32)
    same_pos = idx[:, None] % NS2 == idx[None, :] % NS2
    w1 = jnp.where(same_pos, spread @ m1 @ spread.T, 0.0).astype(BF16)
    s2 = jnp.arange(DFT_B, dtype=jnp.int32).reshape(DFT_B // NS2, 1, NS2)
    ang = _angles(a[None, :, None] * s2, SEQ).reshape(DFT_B // NS2, n, 1)
    t1 = (w1, jnp.cos(ang), -jnp.sin(ang))

    b = jnp.arange(DFT_B, dtype=jnp.int32)
    ang = _angles(b[:, None] * b[None, :], DFT_B)
    w2 = jnp.concatenate([jnp.cos(ang), jnp.sin(ang)], axis=1).astype(BF16)

    n = jnp.arange(CTX_LEN, dtype=jnp.int32)
    ang = _angles(n[:, None] * n[None, :], CTX_LEN)
    wctx = jnp.concatenate([jnp.cos(ang), jnp.sin(ang)], axis=1).astype(BF16)
    return wc, t1, w2, wctx


def kernel(x, c, ctx, c_ctx, w_mod, b_mod, norm_g, ffn_w_in, ffn_w_out, w_in, b_gate,
           lam_params, subln_g, w_attn_out, w_four_out, w_o, final_g):
    xs = x.reshape(R_LAT, D_MODEL)
    x_ctx = ctx.reshape(BATCH * CTX_LEN, D_MODEL)
    conds = jnp.concatenate([c, c_ctx[None, :], jnp.zeros((8 - BATCH - 1, D_MODEL), F32)], axis=0)
    m_all = _modulation(conds, w_mod, b_mod)
    tabs = _rope_tables()
    wc, t1, w2, wctx = _dft_tables()

    w_ffn_in, w_ffn_out = ffn_w_in.astype(BF16), ffn_w_out.astype(BF16)
    w_in_b = w_in.astype(BF16)
    wa, wf, wo = w_attn_out.astype(BF16), w_four_out.astype(BF16), w_o.astype(BF16)

    for l in range(DEPTH):
        last = l == DEPTH - 1
        lam_init = 0.8 - 0.6 * math.exp(-0.3 * l)
        m_l = m_all[l:l + 1]
        xs = _ffn(xs, m_l, norm_g[l, 0], w_ffn_in, w_ffn_out, wsel=(l, 0), sub=0, n_tiles=N_ALL_TILES,
                  x_ctx=x_ctx if l == 0 else None)
        q, k, vt, f, ga, gf = _proj(xs, m_l, norm_g[l, 1], w_in_b, tabs, wc, b_gate[l], layer=l)
        attn, attn_ctx = _attention(q, k, vt, lam_params[l], subln_g[l], lam_init=lam_init,
                                    with_ctx_queries=not last)
        four = _fourier_latent(f, t1, w2)
        four_ctx = None if last else _fourier_ctx(f, wctx)
        n_tiles = N_LAT_TILES if last else N_ALL_TILES
        xs = _merge(xs, m_l, attn, four, ga, gf, wa, wf, wo, layer=l, n_tiles=n_tiles,
                    attn_ctx=attn_ctx, four_ctx=four_ctx)
        xs = _ffn(xs, m_l, norm_g[l, 2], w_ffn_in, w_ffn_out, wsel=(l, 1), sub=2, n_tiles=n_tiles,
                  final_g=final_g if last else None)
    return xs.reshape(BATCH, SEQ, D_MODEL)
```

```python
import functools
import math

import jax
import jax.numpy as jnp
from jax import lax
from jax.experimental import pallas as pl
from jax.experimental.pallas import tpu as pltpu

D_MODEL = 1024
BATCH = 2
SEQ = 8192
DEPTH = 2
CTX_LEN = 256
GRID_W = 64
N_HEADS = 8
HEAD_DIM = 64
V_DIM = 2 * HEAD_DIM
QK_W = N_HEADS * 2 * HEAD_DIM
ATTN_WIDTH = N_HEADS * V_DIM
N_FGROUPS = 4
FGROUP_DIM = 128
FOURIER_WIDTH = N_FGROUPS * FGROUP_DIM
F_OFF = 2 * QK_W + ATTN_WIDTH
G_OFF = F_OFF + FOURIER_WIDTH
IN_WIDTH = G_OFF + 2 * D_MODEL
D_FF = 2816
AXIS_DIM = HEAD_DIM // 2
ROPE_BASE = 10000.0
N_MOD = 9
EPS = 1e-6

R_LAT = BATCH * SEQ
R_ALL = R_LAT + BATCH * CTX_LEN
TM = 512
N_LAT_TILES = R_LAT // TM
N_ALL_TILES = R_ALL // TM
FF_CHUNK = 256
N_FF_CHUNKS = D_FF // FF_CHUNK
TQ = 256
TK = 8192
N_LAT_BLOCKS = SEQ // TK
PROB_SUM_LIMIT = 2.0 ** 24
N_LAT_QT = SEQ // TQ
DFT_A = 64
DFT_B = 128
NS2 = 16
NK1 = 8
VMEM_LIMIT = 56 * 1024 * 1024

F32 = jnp.float32
BF16 = jnp.bfloat16


def _dot(a, b):
    return jnp.dot(a, b, preferred_element_type=F32)


def _sigmoid(x):
    return 1.0 / (1.0 + jnp.exp(-x))


def _mod_row(i):
    return jnp.minimum(i // (SEQ // TM), 2)


def _norm_mod(x, g, shift, scale):
    var = jnp.mean(x * x, axis=-1, keepdims=True)
    return x * lax.rsqrt(var + EPS) * (g * (1.0 + scale)) + shift


def _mod_kernel(c_ref, w_ref, b_ref, o_ref):
    c = c_ref[...]
    s = c * _sigmoid(c)
    o_ref[0] = jnp.dot(s, w_ref[0], preferred_element_type=F32,
                       precision=lax.Precision.HIGHEST) + b_ref[0]


def _modulation(conds, w_mod, b_mod):
    tn = D_MODEL
    out = pl.pallas_call(
        _mod_kernel,
        grid=(DEPTH, N_MOD),
        in_specs=[
            pl.BlockSpec((8, D_MODEL), lambda l, j: (0, 0)),
            pl.BlockSpec((1, D_MODEL, tn), lambda l, j: (l, 0, j)),
            pl.BlockSpec((1, 1, tn), lambda l, j: (l, 0, j)),
        ],
        out_specs=pl.BlockSpec((1, 8, tn), lambda l, j: (l, 0, j)),
        out_shape=jax.ShapeDtypeStruct((DEPTH, 8, N_MOD * D_MODEL), F32),
        compiler_params=pltpu.CompilerParams(
            dimension_semantics=("arbitrary", "arbitrary"), vmem_limit_bytes=VMEM_LIMIT),
        name="modulation",
    )(conds, w_mod, b_mod.reshape(DEPTH, 1, N_MOD * D_MODEL))
    return out.reshape(DEPTH, 8, N_MOD, D_MODEL)


def _ffn_kernel(*refs, sub, final, split_input):
    refs = list(refs)
    x_ref = refs.pop(0)
    xc_ref = refs.pop(0) if split_input else None
    m_ref, g_ref, win_ref, wout_ref = refs[:4]
    fg_ref = refs[4] if final else None
    o_ref, u_ref, h_ref = refs[-3:]
    x = x_ref[...]
    if split_input:
        x = jnp.where(pl.program_id(0) < N_LAT_TILES, x, xc_ref[...])
    m = m_ref[0, 0]
    shift = m[3 * sub:3 * sub + 1]
    scale = m[3 * sub + 1:3 * sub + 2]
    gate = m[3 * sub + 2:3 * sub + 3]
    u_ref[...] = _norm_mod(x, g_ref[...], shift, scale).astype(BF16)
    for c in range(N_FF_CHUNKS):
        lo = c * FF_CHUNK
        a = _dot(u_ref[...], win_ref[0, 0, :, lo:lo + FF_CHUNK])
        b = _dot(u_ref[...], win_ref[0, 0, :, D_FF + lo:D_FF + lo + FF_CHUNK])
        h_ref[:, lo:lo + FF_CHUNK] = (a * _sigmoid(a) * b).astype(BF16)
    y = x + 0.5 * gate * _dot(h_ref[...], wout_ref[0, 0])
    if final:
        var = jnp.mean(y * y, axis=-1, keepdims=True)
        y = y * lax.rsqrt(var + EPS) * fg_ref[...]
    o_ref[...] = y


def _ffn(x, m_l, g, w_in_c, w_out_c, *, wsel, sub, n_tiles, final_g=None, x_ctx=None):
    final = final_g is not None
    split_input = x_ctx is not None
    const = dict(pipeline_mode=pl.Buffered(1))
    if split_input:
        in_specs = [pl.BlockSpec((TM, D_MODEL), lambda i: (jnp.minimum(i, N_LAT_TILES - 1), 0)),
                    pl.BlockSpec((TM, D_MODEL), lambda i: (0, 0))]
        args = [x, x_ctx]
    else:
        in_specs = [pl.BlockSpec((TM, D_MODEL), lambda i: (i, 0))]
        args = [x]
    in_specs += [
        pl.BlockSpec((1, 1, N_MOD, D_MODEL), lambda i: (0, _mod_row(i), 0, 0)),
        pl.BlockSpec((1, D_MODEL), lambda i: (0, 0)),
        pl.BlockSpec((1, 1, D_MODEL, 2 * D_FF), lambda i: wsel + (0, 0), **const),
        pl.BlockSpec((1, 1, D_FF, D_MODEL), lambda i: wsel + (0, 0), **const),
    ]
    args += [m_l, g.reshape(1, D_MODEL), w_in_c, w_out_c]
    if final:
        in_specs.append(pl.BlockSpec((1, D_MODEL), lambda i: (0, 0)))
        args.append(final_g.reshape(1, D_MODEL))
    return pl.pallas_call(
        functools.partial(_ffn_kernel, sub=sub, final=final, split_input=split_input),
        grid=(n_tiles,),
        in_specs=in_specs,
        out_specs=pl.BlockSpec((TM, D_MODEL), lambda i: (i, 0)),
        out_shape=jax.ShapeDtypeStruct((n_tiles * TM, D_MODEL), F32),
        scratch_shapes=[pltpu.VMEM((TM, D_MODEL), BF16), pltpu.VMEM((TM, D_FF), BF16)],
        compiler_params=pltpu.CompilerParams(
            dimension_semantics=("arbitrary",), vmem_limit_bytes=VMEM_LIMIT),
        name="ffn",
    )(*args)


def _proj_kernel(x_ref, m_ref, g_ref, w_ref, rrow_ref, rcol_ref, wc_ref, bg_ref,
                 q_ref, k_ref, vt_ref, f_ref, ga_ref, gf_ref, u_ref):
    x = x_ref[...]
    m = m_ref[0, 0]
    u_ref[...] = _norm_mod(x, g_ref[...], m[3:4], m[4:5]).astype(BF16)
    cw = 256

    def rope_table(t):
        per_row = rrow_ref[t]
        rows = jnp.broadcast_to(per_row[:, None, :], (TM // GRID_W, GRID_W, 128)).reshape(TM, 128)
        tab = rows + rcol_ref[0, t]
        return jnp.concatenate([tab] * (cw // 128), axis=1)

    cos, sa, sb = rope_table(0), rope_table(1), rope_table(2)

    def col(off):
        return _dot(u_ref[...], w_ref[0, :, off:off + cw])

    for out_ref, base, qscale in ((q_ref, 0, HEAD_DIM ** -0.5 * math.log2(math.e)), (k_ref, QK_W, 1.0)):
        for j in range(QK_W // cw):
            p = col(base + j * cw)
            r = p * cos + pltpu.roll(p, cw - 16, 1) * sa + pltpu.roll(p, 16, 1) * sb
            if qscale != 1.0:
                r = r * qscale
            out_ref[:, j * cw:(j + 1) * cw] = r.astype(BF16)
    for j in range(ATTN_WIDTH // cw):
        vt_ref[j * cw:(j + 1) * cw, :] = col(2 * QK_W + j * cw).T.astype(BF16)
    for j in range(FOURIER_WIDTH // cw):
        p = col(F_OFF + j * cw).astype(BF16)
        for t in range(cw // FGROUP_DIM):
            grp = j * (cw // FGROUP_DIM) + t
            r = _dot(p[:, t * FGROUP_DIM:(t + 1) * FGROUP_DIM], wc_ref[...])
            shape3 = (TM // DFT_B, DFT_B, FGROUP_DIM)
            f_ref[:, :, grp * FGROUP_DIM:(grp + 1) * FGROUP_DIM] = r[:, :FGROUP_DIM].astype(BF16).reshape(shape3)
            f_ref[:, :, FOURIER_WIDTH + grp * FGROUP_DIM:FOURIER_WIDTH + (grp + 1) * FGROUP_DIM] = (
                r[:, FGROUP_DIM:].astype(BF16).reshape(shape3))
    for out_ref, base, brow in ((ga_ref, G_OFF, 0), (gf_ref, G_OFF + D_MODEL, 1)):
        for j in range(D_MODEL // cw):
            p = col(base + j * cw) + bg_ref[brow:brow + 1, j * cw:(j + 1) * cw]
            out_ref[:, j * cw:(j + 1) * cw] = _sigmoid(p).astype(BF16)


def _proj(x, m_l, g, w_in_b, tabs, wc, b_gate, *, layer):
    rope_rows, rope_cols = tabs
    const = dict(pipeline_mode=pl.Buffered(1))
    row = lambda i: (i, 0)
    fix = lambda i: (0, 0)
    wide = jax.ShapeDtypeStruct((R_ALL, D_MODEL), BF16)
    row_spec = pl.BlockSpec((TM, D_MODEL), row)
    return pl.pallas_call(
        _proj_kernel,
        grid=(N_ALL_TILES,),
        in_specs=[
            pl.BlockSpec((TM, D_MODEL), row),
            pl.BlockSpec((1, 1, N_MOD, D_MODEL), lambda i: (0, _mod_row(i), 0, 0)),
            pl.BlockSpec((1, D_MODEL), fix),
            pl.BlockSpec((1, D_MODEL, IN_WIDTH), lambda i: (layer, 0, 0), **const),
            pl.BlockSpec((3, TM // GRID_W, 128), lambda i: (0, i, 0)),
            pl.BlockSpec((1, 3, TM, 128), lambda i: (i // N_LAT_TILES, 0, 0, 0)),
            pl.BlockSpec((FGROUP_DIM, 2 * FGROUP_DIM), fix),
            pl.BlockSpec((2, D_MODEL), fix),
        ],
        out_specs=[row_spec, row_spec, pl.BlockSpec((ATTN_WIDTH, TM), lambda i: (0, i)),
                   pl.BlockSpec((TM // DFT_B, DFT_B, 2 * FOURIER_WIDTH), lambda i: (i, 0, 0)), row_spec, row_spec],
        out_shape=[wide, wide, jax.ShapeDtypeStruct((ATTN_WIDTH, R_ALL), BF16),
                   jax.ShapeDtypeStruct((R_ALL // DFT_B, DFT_B, 2 * FOURIER_WIDTH), BF16), wide, wide],
        scratch_shapes=[pltpu.VMEM((TM, D_MODEL), BF16)],
        compiler_params=pltpu.CompilerParams(
            dimension_semantics=("arbitrary",), vmem_limit_bytes=VMEM_LIMIT),
        name="proj",
    )(x, m_l, g.reshape(1, D_MODEL), w_in_b, rope_rows, rope_cols, wc, b_gate)


def _masked_queries(q):
    lane = lax.broadcasted_iota(jnp.int32, q.shape, 1)
    zero = jnp.zeros_like(q)
    return jnp.where(lane < HEAD_DIM, q, zero), jnp.where(lane >= HEAD_DIM, q, zero)


def _lambda(lp, lam_init):
    return (jnp.exp(jnp.sum(lp[0:1] * lp[1:2], axis=-1, keepdims=True))
            - jnp.exp(jnp.sum(lp[2:3] * lp[3:4], axis=-1, keepdims=True)) + lam_init)


def _attn_epilogue(acc, l, lam, sg, lam_init):
    n = acc.shape[1] // 2
    o = acc / l
    d = o[:, :n] - lam * o[:, n:]
    var = jnp.mean(d * d, axis=0, keepdims=True)
    y = d * lax.rsqrt(var + EPS) * sg * (1.0 - lam_init)
    return y.T.astype(BF16)


def _attn_kernel(lp_ref, sg_ref, q_ref, kc_ref, vc_ref, kl_ref, vl_ref, o_ref,
                 qs_ref, pacc_ref, pl_ref, qx_ref, m_ref, l_ref, acc_ref, *, lam_init):
    lam = _lambda(lp_ref[...], lam_init)

    def tile_rows(t):
        if isinstance(t, int):
            return pl.ds(t * TQ, TQ)
        return pl.ds(pl.multiple_of(t * TQ, TQ), TQ)

    def stage_queries(dst, t):
        q0, q1 = _masked_queries(q_ref[tile_rows(t), :])
        dst[0:TQ, :] = q0
        dst[TQ:, :] = q1

    def scores(keys, queries_ref):
        return lax.dot_general(keys, queries_ref[...], (((1,), (1,)), ((), ())), preferred_element_type=F32)

    def fold8(fn, a):
        return fn(a.reshape(a.shape[0] // 8, 8, 2 * TQ), axis=0)

    def latent_block(blk):
        return kl_ref[blk * TK:(blk + 1) * TK, :], vl_ref[:, blk * TK:(blk + 1) * TK]

    def store_tile(t, acc, l8):
        l = jnp.sum(l8, axis=0, keepdims=True)
        o_ref[tile_rows(t), :] = _attn_epilogue(acc, l, lam, sg_ref[...], lam_init)

    def exact_tile(t):
        stage_queries(qx_ref, t)
        m_ref[...] = jnp.full_like(m_ref, -jnp.inf)
        l_ref[...] = jnp.zeros_like(l_ref)
        acc_ref[...] = jnp.zeros_like(acc_ref)

        def update(keys, vt):
            s = scores(keys, qx_ref)
            m_old = m_ref[...]
            m_new = jnp.maximum(m_old, jnp.max(s, axis=0, keepdims=True))
            alpha = jnp.exp2(m_old - m_new)
            p = jnp.exp2(s - m_new)
            m_ref[...] = m_new
            l_ref[...] = alpha * l_ref[...] + fold8(jnp.sum, p)
            acc_ref[...] = alpha * acc_ref[...] + _dot(vt, p.astype(BF16))

        def latent(blk, carry):
            off = pl.multiple_of(blk * TK, TK)
            update(kl_ref[pl.ds(off, TK), :], vl_ref[:, pl.ds(off, TK)])
            return carry

        update(kc_ref[...], vc_ref[...])
        lax.fori_loop(0, N_LAT_BLOCKS, latent, 0)
        store_tile(t, acc_ref[...], l_ref[...])

    def tile(t, redo_prev):
        stage_queries(qs_ref, t)
        s = scores(kc_ref[...], qs_ref)
        m0 = jnp.max(s, axis=0, keepdims=True)
        p = jnp.exp2(s - m0)
        l8 = fold8(jnp.sum, p)
        acc = _dot(vc_ref[...], p.astype(BF16))
        for blk in range(N_LAT_BLOCKS):
            keys, vt = latent_block(blk)
            p = jnp.exp2(scores(keys, qs_ref) - m0)
            l8 = l8 + fold8(jnp.sum, p)
            acc = acc + _dot(vt, p.astype(BF16))
        prev = jnp.maximum(t - 1, 0)
        store_tile(prev, pacc_ref[...], pl_ref[...])
        pacc_ref[...] = acc
        pl_ref[...] = l8
        redo = (jnp.max(l8) > PROB_SUM_LIMIT).astype(jnp.int32)

        @pl.when(redo_prev == 1)
        def _():
            exact_tile(prev)

        return redo

    pacc_ref[...] = jnp.ones_like(pacc_ref)
    pl_ref[...] = jnp.ones_like(pl_ref)
    redo_last = lax.fori_loop(0, N_LAT_QT, tile, jnp.int32(0))
    store_tile(N_LAT_QT - 1, pacc_ref[...], pl_ref[...])

    @pl.when(redo_last == 1)
    def _():
        exact_tile(N_LAT_QT - 1)


def _attn_ctx_kernel(lp_ref, sg_ref, q_ref, k_ref, vt_ref, o_ref, *, lam_init):
    q0, q1 = _masked_queries(q_ref[...])
    qs = jnp.concatenate([q0, q1], axis=0)
    s = lax.dot_general(k_ref[...], qs, (((1,), (1,)), ((), ())), preferred_element_type=F32)
    p = jnp.exp2(s - jnp.max(s, axis=0, keepdims=True))
    l = jnp.sum(p, axis=0, keepdims=True)
    acc = _dot(vt_ref[...], p.astype(BF16))
    o_ref[...] = _attn_epilogue(acc, l, _lambda(lp_ref[...], lam_init), sg_ref[...], lam_init)


def _attention(q, k, vt, lam_params_l, subln_g_l, *, lam_init, with_ctx_queries):
    ctx_blk0 = R_LAT // CTX_LEN
    sg = subln_g_l.reshape(V_DIM, 1)
    attn = pl.pallas_call(
        functools.partial(_attn_kernel, lam_init=lam_init),
        grid=(BATCH, N_HEADS),
        in_specs=[
            pl.BlockSpec((4, HEAD_DIM), lambda b, h: (0, 0)),
            pl.BlockSpec((V_DIM, 1), lambda b, h: (0, 0)),
            pl.BlockSpec((SEQ, V_DIM), lambda b, h: (b, h)),
            pl.BlockSpec((CTX_LEN, V_DIM), lambda b, h: (ctx_blk0 + b, h)),
            pl.BlockSpec((V_DIM, CTX_LEN), lambda b, h: (h, ctx_blk0 + b)),
            pl.BlockSpec((SEQ, V_DIM), lambda b, h: (b, h)),
            pl.BlockSpec((V_DIM, SEQ), lambda b, h: (h, b)),
        ],
        out_specs=pl.BlockSpec((SEQ, V_DIM), lambda b, h: (b, h)),
        out_shape=jax.ShapeDtypeStruct((R_LAT, ATTN_WIDTH), BF16),
        scratch_shapes=[
            pltpu.VMEM((2 * TQ, V_DIM), BF16),
            pltpu.VMEM((V_DIM, 2 * TQ), F32),
            pltpu.VMEM((8, 2 * TQ), F32),
            pltpu.VMEM((2 * TQ, V_DIM), BF16),
            pltpu.VMEM((1, 2 * TQ), F32),
            pltpu.VMEM((8, 2 * TQ), F32),
            pltpu.VMEM((V_DIM, 2 * TQ), F32),
        ],
        compiler_params=pltpu.CompilerParams(
            dimension_semantics=("arbitrary", "arbitrary"), vmem_limit_bytes=VMEM_LIMIT),
        name="diff_attn",
    )(lam_params_l, sg, q, k, vt, k, vt)
    if not with_ctx_queries:
        return attn, None
    attn_ctx = pl.pallas_call(
        functools.partial(_attn_ctx_kernel, lam_init=lam_init),
        grid=(BATCH, N_HEADS),
        in_specs=[
            pl.BlockSpec((4, HEAD_DIM), lambda b, h: (0, 0)),
            pl.BlockSpec((V_DIM, 1), lambda b, h: (0, 0)),
            pl.BlockSpec((CTX_LEN, V_DIM), lambda b, h: (ctx_blk0 + b, h)),
            pl.BlockSpec((CTX_LEN, V_DIM), lambda b, h: (ctx_blk0 + b, h)),
            pl.BlockSpec((V_DIM, CTX_LEN), lambda b, h: (h, ctx_blk0 + b)),
        ],
        out_specs=pl.BlockSpec((CTX_LEN, V_DIM), lambda b, h: (b, h)),
        out_shape=jax.ShapeDtypeStruct((BATCH * CTX_LEN, ATTN_WIDTH), BF16),
        compiler_params=pltpu.CompilerParams(
            dimension_semantics=("arbitrary", "arbitrary"), vmem_limit_bytes=VMEM_LIMIT),
        name="diff_attn_ctx",
    )(lam_params_l, sg, q, k, vt)
    return attn, attn_ctx


def _dft1_kernel(x_ref, w_ref, twr_ref, twi_ref, o_ref):
    n = DFT_A * NS2
    x = x_ref[...].reshape(n, 2 * FOURIER_WIDTH)
    xs = jnp.concatenate([x[:, :FOURIER_WIDTH], x[:, FOURIER_WIDTH:]], axis=0)
    r = _dot(w_ref[...], xs)
    re, im = r[:n], r[n:]
    twr, twi = twr_ref[0], twi_ref[0]
    o_ref[:, :, :FOURIER_WIDTH] = (re * twr - im * twi).astype(BF16).reshape(DFT_A, NS2, FOURIER_WIDTH)
    o_ref[:, :, FOURIER_WIDTH:] = (re * twi + im * twr).astype(BF16).reshape(DFT_A, NS2, FOURIER_WIDTH)


def _dft2_kernel(x_ref, w_ref, o_ref, *, n, nk, scale):
    g = n // DFT_B
    for j in range(nk):
        x = x_ref[j * g:(j + 1) * g].reshape(n, 2 * FOURIER_WIDTH)
        xs = jnp.concatenate([x[:, :FOURIER_WIDTH], x[:, FOURIER_WIDTH:]], axis=0)
        r = _dot(w_ref[...], xs) * scale
        o_ref[:, j * FOURIER_WIDTH:(j + 1) * FOURIER_WIDTH] = r.astype(BF16)


def _fourier_latent(f_all, t1, w2):
    w1, twr, twi = t1
    blk = (DFT_A, NS2, 2 * FOURIER_WIDTH)
    n = DFT_A * NS2
    x2 = pl.pallas_call(
        _dft1_kernel,
        grid=(BATCH, DFT_B // NS2),
        in_specs=[
            pl.BlockSpec(blk, lambda b, j: (b, j, 0)),
            pl.BlockSpec((2 * n, 2 * n), lambda b, j: (0, 0)),
            pl.BlockSpec((1, n, 1), lambda b, j: (j, 0, 0)),
            pl.BlockSpec((1, n, 1), lambda b, j: (j, 0, 0)),
        ],
        out_specs=pl.BlockSpec(blk, lambda b, j: (b, j, 0)),
        out_shape=jax.ShapeDtypeStruct((BATCH * DFT_A, DFT_B, 2 * FOURIER_WIDTH), BF16),
        compiler_params=pltpu.CompilerParams(
            dimension_semantics=("arbitrary", "arbitrary"), vmem_limit_bytes=VMEM_LIMIT),
        name="dft_stage1",
    )(f_all, w1, twr, twi)
    y = pl.pallas_call(
        functools.partial(_dft2_kernel, n=DFT_B, nk=NK1, scale=(SEQ * FGROUP_DIM) ** -0.5),
        grid=(BATCH, DFT_A // NK1),
        in_specs=[
            pl.BlockSpec((NK1, DFT_B, 2 * FOURIER_WIDTH), lambda b, j: (b * (DFT_A // NK1) + j, 0, 0)),
            pl.BlockSpec((DFT_B, 2 * DFT_B), lambda b, j: (0, 0)),
        ],
        out_specs=pl.BlockSpec((DFT_B, NK1 * FOURIER_WIDTH), lambda b, j: (b, j)),
        out_shape=jax.ShapeDtypeStruct((R_LAT // DFT_A, DFT_A * FOURIER_WIDTH), BF16),
        compiler_params=pltpu.CompilerParams(
            dimension_semantics=("arbitrary", "arbitrary"), vmem_limit_bytes=VMEM_LIMIT),
        name="dft_stage2",
    )(x2, w2)
    return y.reshape(R_LAT, FOURIER_WIDTH)


def _fourier_ctx(f_all, wctx):
    ctx_blk0 = R_LAT // CTX_LEN
    return pl.pallas_call(
        functools.partial(_dft2_kernel, n=CTX_LEN, nk=1, scale=(CTX_LEN * FGROUP_DIM) ** -0.5),
        grid=(BATCH,),
        in_specs=[
            pl.BlockSpec((CTX_LEN // DFT_B, DFT_B, 2 * FOURIER_WIDTH), lambda b: (ctx_blk0 + b, 0, 0)),
            pl.BlockSpec((CTX_LEN, 2 * CTX_LEN), lambda b: (0, 0)),
        ],
        out_specs=pl.BlockSpec((CTX_LEN, FOURIER_WIDTH), lambda b: (b, 0)),
        out_shape=jax.ShapeDtypeStruct((BATCH * CTX_LEN, FOURIER_WIDTH), BF16),
        compiler_params=pltpu.CompilerParams(
            dimension_semantics=("arbitrary",), vmem_limit_bytes=VMEM_LIMIT),
        name="dft_ctx",
    )(f_all, wctx)


def _merge_kernel(*refs, split_input):
    refs = list(refs)
    x_ref, m_ref, a_ref, f_ref = refs[:4]
    ac_ref, fc_ref = (refs[4], refs[5]) if split_input else (None, None)
    ga_ref, gf_ref, wa_ref, wf_ref, wo_ref, o_ref = refs[-6:]
    attn, four = a_ref[...], f_ref[...]
    if split_input:
        latent = pl.program_id(0) < N_LAT_TILES
        attn = jnp.where(latent, attn, ac_ref[...])
        four = jnp.where(latent, four, fc_ref[...])
    a = _dot(attn, wa_ref[0])
    fr = _dot(four, wf_ref[0])
    merged = ga_ref[...].astype(F32) * a + gf_ref[...].astype(F32) * fr
    mix = _dot(merged.astype(BF16), wo_ref[0])
    o_ref[...] = x_ref[...] + m_ref[0, 0][5:6] * mix


def _merge(x, m_l, attn, four, ga, gf, wa, wf, wo, *, layer, n_tiles, attn_ctx=None, four_ctx=None):
    split_input = attn_ctx is not None
    const = dict(pipeline_mode=pl.Buffered(1))
    row = lambda i: (i, 0)
    lat_row = (lambda i: (jnp.minimum(i, N_LAT_TILES - 1), 0)) if split_input else row
    fix = lambda i: (0, 0)
    wsel = lambda i: (layer, 0, 0)
    in_specs = [
        pl.BlockSpec((TM, D_MODEL), row),
        pl.BlockSpec((1, 1, N_MOD, D_MODEL), lambda i: (0, _mod_row(i), 0, 0)),
        pl.BlockSpec((TM, ATTN_WIDTH), lat_row),
        pl.BlockSpec((TM, FOURIER_WIDTH), lat_row),
    ]
    args = [x, m_l, attn, four]
    if split_input:
        in_specs += [pl.BlockSpec((TM, ATTN_WIDTH), fix), pl.BlockSpec((TM, FOURIER_WIDTH), fix)]
        args += [attn_ctx, four_ctx]
    in_specs += [
        pl.BlockSpec((TM, D_MODEL), row),
        pl.BlockSpec((TM, D_MODEL), row),
        pl.BlockSpec((1, ATTN_WIDTH, D_MODEL), wsel, **const),
        pl.BlockSpec((1, FOURIER_WIDTH, D_MODEL), wsel, **const),
        pl.BlockSpec((1, D_MODEL, D_MODEL), wsel, **const),
    ]
    args += [ga, gf, wa, wf, wo]
    return pl.pallas_call(
        functools.partial(_merge_kernel, split_input=split_input),
        grid=(n_tiles,),
        in_specs=in_specs,
        out_specs=pl.BlockSpec((TM, D_MODEL), row),
        out_shape=jax.ShapeDtypeStruct((n_tiles * TM, D_MODEL), F32),
        compiler_params=pltpu.CompilerParams(
            dimension_semantics=("arbitrary",), vmem_limit_bytes=VMEM_LIMIT),
        name="merge",
    )(*args)


def _rope_tables():
    lane = jnp.arange(128, dtype=jnp.int32) % HEAD_DIM
    inv = 1.0 / (ROPE_BASE ** ((2 * (lane % (AXIS_DIM // 2))).astype(F32) / AXIS_DIM))
    row_lane = lane < AXIS_DIM
    first = (lane % AXIS_DIM) < (AXIS_DIM // 2)

    def tables(pos, lanes):
        ang = pos.astype(F32)[:, None] * inv[None, :]
        cos, sin = jnp.cos(ang), jnp.sin(ang)
        tabs = jnp.stack([cos, jnp.where(first[None, :], -sin, 0.0), jnp.where(first[None, :], 0.0, sin)])
        return jnp.where(lanes[None, None, :], tabs, 0.0)

    identity = jnp.stack([jnp.ones((128,), F32), jnp.zeros((128,), F32), jnp.zeros((128,), F32)])[:, None, :]
    lat_rows = tables(jnp.arange(SEQ // GRID_W, dtype=jnp.int32), row_lane)
    ctx_rows = jnp.broadcast_to(jnp.where(row_lane[None, None, :], identity, 0.0),
                                (3, BATCH * CTX_LEN // GRID_W, 128))
    rope_rows = jnp.concatenate([lat_rows] * BATCH + [ctx_rows], axis=1)
    lat_cols = tables(jnp.arange(TM, dtype=jnp.int32) % GRID_W, ~row_lane)
    ctx_cols = jnp.broadcast_to(jnp.where(~row_lane[None, None, :], identity, 0.0), (3, TM, 128))
    return rope_rows, jnp.stack([lat_cols, ctx_cols])


def _angles(num, den):
    return (2.0 * math.pi / den) * (num % den).astype(F32)


def _dft_tables():
    c = jnp.arange(FGROUP_DIM, dtype=jnp.int32)
    ang = _angles(c[:, None] * c[None, :], FGROUP_DIM)
    wc = jnp.concatenate([jnp.cos(ang), -jnp.sin(ang)], axis=1).astype(BF16)

    n = DFT_A * NS2
    a = jnp.arange(DFT_A, dtype=jnp.int32)
    ang = _angles(a[:, None] * a[None, :], DFT_A)
    cs, sn = jnp.cos(ang), jnp.sin(ang)
    m1 = jnp.concatenate([jnp.concatenate([cs, sn], axis=1),
                          jnp.concatenate([-sn, cs], axis=1)], axis=0)
    idx = jnp.arange(2 * n, dtype=jnp.int32)
    spread = (idx[:, None] // NS2 == jnp.arange(2 * DFT_A, dtype=jnp.int32)[None, :]).astype(F32)
    same_pos = idx[:, None] % NS2 == idx[None, :] % NS2
    w1 = jnp.where(same_pos, spread @ m1 @ spread.T, 0.0).astype(BF16)
    s2 = jnp.arange(DFT_B, dtype=jnp.int32).reshape(DFT_B // NS2, 1, NS2)
    ang = _angles(a[None, :, None] * s2, SEQ).reshape(DFT_B // NS2, n, 1)
    t1 = (w1, jnp.cos(ang), -jnp.sin(ang))

    b = jnp.arange(DFT_B, dtype=jnp.int32)
    ang = _angles(b[:, None] * b[None, :], DFT_B)
    w2 = jnp.concatenate([jnp.cos(ang), jnp.sin(ang)], axis=1).astype(BF16)

    n = jnp.arange(CTX_LEN, dtype=jnp.int32)
    ang = _angles(n[:, None] * n[None, :], CTX_LEN)
    wctx = jnp.concatenate([jnp.cos(ang), jnp.sin(ang)], axis=1).astype(BF16)
    return wc, t1, w2, wctx


def kernel(x, c, ctx, c_ctx, w_mod, b_mod, norm_g, ffn_w_in, ffn_w_out, w_in, b_gate,
           lam_params, subln_g, w_attn_out, w_four_out, w_o, final_g):
    xs = x.reshape(R_LAT, D_MODEL)
    x_ctx = ctx.reshape(BATCH * CTX_LEN, D_MODEL)
    conds = jnp.concatenate([c, c_ctx[None, :], jnp.zeros((8 - BATCH - 1, D_MODEL), F32)], axis=0)
    m_all = _modulation(conds, w_mod, b_mod)
    tabs = _rope_tables()
    wc, t1, w2, wctx = _dft_tables()

    w_ffn_in, w_ffn_out = ffn_w_in.astype(BF16), ffn_w_out.astype(BF16)
    w_in_b = w_in.astype(BF16)
    wa, wf, wo = w_attn_out.astype(BF16), w_four_out.astype(BF16), w_o.astype(BF16)

    for l in range(DEPTH):
        last = l == DEPTH - 1
        lam_init = 0.8 - 0.6 * math.exp(-0.3 * l)
        m_l = m_all[l:l + 1]
        xs = _ffn(xs, m_l, norm_g[l, 0], w_ffn_in, w_ffn_out, wsel=(l, 0), sub=0, n_tiles=N_ALL_TILES,
                  x_ctx=x_ctx if l == 0 else None)
        q, k, vt, f, ga, gf = _proj(xs, m_l, norm_g[l, 1], w_in_b, tabs, wc, b_gate[l], layer=l)
        attn, attn_ctx = _attention(q, k, vt, lam_params[l], subln_g[l], lam_init=lam_init,
                                    with_ctx_queries=not last)
        four = _fourier_latent(f, t1, w2)
        four_ctx = None if last else _fourier_ctx(f, wctx)
        n_tiles = N_LAT_TILES if last else N_ALL_TILES
        xs = _merge(xs, m_l, attn, four, ga, gf, wa, wf, wo, layer=l, n_tiles=n_tiles,
                    attn_ctx=attn_ctx, four_ctx=four_ctx)
        xs = _ffn(xs, m_l, norm_g[l, 2], w_ffn_in, w_ffn_out, wsel=(l, 1), sub=2, n_tiles=n_tiles,
                  final_g=final_g if last else None)
    return xs.reshape(BATCH, SEQ, D_MODEL)
```
